```python
import math
import jax, jax.numpy as jnp
from jax import lax
import numpy as np

D_MODEL = 1024
BATCH = 16
SEQ = 2048
DEPTH = 4

D_FF = 2816
NORM_EPS = 1e-6

HEAD_DIM = 64
ATT_PATTERNS = ((128, 1), (512, 4), (2048, 16))
N_ATT_GROUPS = len(ATT_PATTERNS)
HEADS_PER_GROUP = 8
N_ATT_HEADS = N_ATT_GROUPS * HEADS_PER_GROUP
ATT_OUT_DIM = HEADS_PER_GROUP * HEAD_DIM
ROPE_THETA = 10000.0

SSM_EXPAND = 2
D_INNER = SSM_EXPAND * D_MODEL
SSM_HEAD_DIM = 64
N_SSM_HEADS = D_INNER // SSM_HEAD_DIM
N_SSM_GROUPS = 4
HEADS_PER_SSM_GROUP = N_SSM_HEADS // N_SSM_GROUPS
D_STATE = 128
CONV_WIDTH = 4
SSD_CHUNK = 128
XBC_DIM = D_INNER + 2 * N_SSM_GROUPS * D_STATE

QKV_COLS = N_ATT_HEADS * HEAD_DIM
IN_SIZES = (QKV_COLS, QKV_COLS, QKV_COLS, D_INNER, XBC_DIM, N_SSM_HEADS, 2 * D_MODEL)
IN_SPLITS = tuple(int(v) for v in np.cumsum(IN_SIZES)[:-1])
N_IN = int(sum(IN_SIZES))

kernel_name = "hybrid_dilated_attn_ssd_macaron"


def rms_norm_f32(x, w):
    xf = x.astype(jnp.float32)
    y = xf * lax.rsqrt(jnp.mean(xf * xf, axis=-1, keepdims=True) + NORM_EPS)
    return y * w.astype(jnp.float32)


def swiglu(h, w_gate, w_up, w_down):
    return (jax.nn.silu(h @ w_gate) * (h @ w_up)) @ w_down


def rope(t, cos, sin):
    half = t.shape[-1] // 2
    t1, t2 = t[..., :half], t[..., half:]
    c = cos[None, :, None, None, :]
    s = sin[None, :, None, None, :]
    return jnp.concatenate([t1 * c - t2 * s, t2 * c + t1 * s], axis=-1)


def dilated_attention(q, k, v, dilation, band):
    bsz, s, h, hd = q.shape
    span = dilation * band
    s_pad = -(-s // span) * span
    sub_len = s_pad // dilation
    nb = sub_len // band

    def to_blocks(t):
        t = jnp.pad(t, ((0, 0), (0, s_pad - s), (0, 0), (0, 0)))
        t = t.reshape(bsz, sub_len, dilation, h, hd)
        t = t.transpose(0, 2, 3, 1, 4)
        return t.reshape(bsz, dilation, h, nb, band, hd)

    def with_prev(t):
        prev = jnp.pad(t, ((0, 0), (0, 0), (0, 0), (1, 0), (0, 0), (0, 0)))[:, :, :, :-1]
        return jnp.concatenate([prev, t], axis=4)

    qb = to_blocks(q)
    kk = with_prev(to_blocks(k))
    vv = with_prev(to_blocks(v))
    scores = jnp.einsum('bdhnqe,bdhnke->bdhnqk', qb, kk) * (hd ** -0.5)
    iq = jnp.arange(band)[:, None]
    ik = jnp.arange(2 * band)[None, :]
    rel = band + iq - ik
    band_ok = (rel >= 0) & (rel <= band)
    key_ok = (jnp.arange(nb)[:, None, None] * band - band + ik[None]) >= 0
    valid = band_ok[None] & key_ok
    scores = jnp.where(valid, scores, -jnp.inf)
    lse = jax.nn.logsumexp(scores, axis=-1)
    p = jnp.exp(scores - lse[..., None])
    o = jnp.einsum('bdhnqk,bdhnke->bdhnqe', p, vv)
    o = o.reshape(bsz, dilation, h, sub_len, hd).transpose(0, 3, 1, 2, 4)
    o = o.reshape(bsz, s_pad, h, hd)[:, :s]
    lse = lse.reshape(bsz, dilation, h, sub_len).transpose(0, 3, 1, 2)
    lse = lse.reshape(bsz, s_pad, h)[:, :s]
    return o, lse


def ssd_chunked_scan(x, dt, a, b_mat, c_mat):
    bsz, s, g, r, p = x.shape
    n = b_mat.shape[-1]
    nc = s // SSD_CHUNK

    def to_chunks(t):
        return jnp.moveaxis(t.reshape(bsz, nc, SSD_CHUNK, *t.shape[2:]), 1, 0)

    idx = jnp.arange(SSD_CHUNK)
    causal = idx[:, None] >= idx[None, :]

    def step(state, inp):
        xc, dtc, bc, cc = inp
        cum = jnp.cumsum(dtc * a, axis=1)
        cum_t = jnp.moveaxis(cum, 1, -1)
        diff = cum_t[..., :, None] - cum_t[..., None, :]
        decay = jnp.exp(jnp.where(causal, diff, -jnp.inf))
        cb = jnp.einsum('bign,bjgn->bgij', cc, bc)
        xdt = xc * dtc[..., None]
        y_diag = jnp.einsum('bgrij,bjgrp->bigrp', cb[:, :, None] * decay, xdt)
        y_off = jnp.einsum('bign,bgrpn->bigrp', cc, state) * jnp.exp(cum)[..., None]
        last = cum[:, -1]
        w_end = jnp.exp(last[:, None] - cum)
        new_state = state * jnp.exp(last)[..., None, None] + jnp.einsum(
            'bjgn,bjgrp->bgrpn', bc, xdt * w_end[..., None])
        return new_state, y_diag + y_off

    state0 = jnp.zeros((bsz, g, r, p, n), jnp.float32)
    xs = (to_chunks(x), to_chunks(dt), to_chunks(b_mat), to_chunks(c_mat))
    _, ys = lax.scan(step, state0, xs)
    return jnp.moveaxis(ys, 0, 1).reshape(bsz, s, g, r, p)


def setup_inputs(seed: int = 0) -> dict:
    key = jax.random.key(seed)
    ks = jax.random.split(key, 24)

    def normal(k, shape, scale):
        return jax.random.normal(k, shape, jnp.float32) * scale

    def gain(k, shape):
        return 1.0 + 0.02 * jax.random.normal(k, shape, jnp.float32)

    dt0 = jnp.exp(jax.random.uniform(ks[13], (DEPTH, N_SSM_HEADS), jnp.float32)
                  * (math.log(0.1) - math.log(0.001)) + math.log(0.001))
    dt0 = jnp.maximum(dt0, 1e-4)
    return {
        'x': normal(ks[0], (BATCH, SEQ, D_MODEL), 1.0),
        'ffn1_norm_w': gain(ks[1], (DEPTH, D_MODEL)),
        'ffn1_w_gate': normal(ks[2], (DEPTH, D_MODEL, D_FF), D_MODEL ** -0.5),
        'ffn1_w_up': normal(ks[3], (DEPTH, D_MODEL, D_FF), D_MODEL ** -0.5),
        'ffn1_w_down': normal(ks[4], (DEPTH, D_FF, D_MODEL), D_FF ** -0.5),
        'mix_norm_w': gain(ks[5], (DEPTH, D_MODEL)),
        'w_in': normal(ks[6], (DEPTH, D_MODEL, N_IN), D_MODEL ** -0.5),
        'b_gates': normal(ks[7], (DEPTH, 2 * D_MODEL), 0.01),
        'q_norm_w': gain(ks[8], (DEPTH, HEAD_DIM)),
        'k_norm_w': gain(ks[9], (DEPTH, HEAD_DIM)),
        'conv_w': normal(ks[10], (DEPTH, CONV_WIDTH, XBC_DIM), CONV_WIDTH ** -0.5),
        'conv_b': normal(ks[11], (DEPTH, XBC_DIM), 0.01),
        'dt_bias': dt0 + jnp.log(-jnp.expm1(-dt0)),
        'a_log': jnp.log(jax.random.uniform(ks[12], (DEPTH, N_SSM_HEADS), jnp.float32, 1.0, 16.0)),
        'd_skip': gain(ks[14], (DEPTH, N_SSM_HEADS)),
        'ssm_norm_w': gain(ks[15], (DEPTH, D_INNER)),
        'w_att_proj': normal(ks[16], (DEPTH, ATT_OUT_DIM, D_MODEL), ATT_OUT_DIM ** -0.5),
        'w_ssm_proj': normal(ks[17], (DEPTH, D_INNER, D_MODEL), D_INNER ** -0.5),
        'w_out': normal(ks[18], (DEPTH, D_MODEL, D_MODEL), D_MODEL ** -0.5),
        'ffn2_norm_w': gain(ks[19], (DEPTH, D_MODEL)),
        'ffn2_w_gate': normal(ks[20], (DEPTH, D_MODEL, D_FF), D_MODEL ** -0.5),
        'ffn2_w_up': normal(ks[21], (DEPTH, D_MODEL, D_FF), D_MODEL ** -0.5),
        'ffn2_w_down': normal(ks[22], (DEPTH, D_FF, D_MODEL), D_FF ** -0.5),
    }


def reference(x, ffn1_norm_w, ffn1_w_gate, ffn1_w_up, ffn1_w_down, mix_norm_w, w_in, b_gates,
              q_norm_w, k_norm_w, conv_w, conv_b, dt_bias, a_log, d_skip, ssm_norm_w,
              w_att_proj, w_ssm_proj, w_out, ffn2_norm_w, ffn2_w_gate, ffn2_w_up, ffn2_w_down):
    bsz, s, _ = x.shape
    dtype = x.dtype
    pos = jnp.arange(s, dtype=jnp.float32)
    inv_freq = 1.0 / (ROPE_THETA ** (jnp.arange(0, HEAD_DIM, 2, dtype=jnp.float32) / HEAD_DIM))
    ang = pos[:, None] * inv_freq[None, :]
    cos, sin = jnp.cos(ang), jnp.sin(ang)

    for i in range(DEPTH):
        h = rms_norm_f32(x, ffn1_norm_w[i]).astype(dtype)
        x = x + (0.5 * swiglu(h, ffn1_w_gate[i], ffn1_w_up[i], ffn1_w_down[i])).astype(dtype)

        h = rms_norm_f32(x, mix_norm_w[i]).astype(dtype)
        proj = h @ w_in[i]
        q, k, v, z, xbc, dt_raw, gate_pre = jnp.split(proj, IN_SPLITS, axis=-1)

        att_shape = (bsz, s, N_ATT_GROUPS, HEADS_PER_GROUP, HEAD_DIM)
        q = rope(rms_norm_f32(q.reshape(att_shape), q_norm_w[i]), cos, sin)
        k = rope(rms_norm_f32(k.reshape(att_shape), k_norm_w[i]), cos, sin)
        v = v.reshape(att_shape).astype(jnp.float32)
        outs, lses = [], []
        for g, (window, dilation) in enumerate(ATT_PATTERNS):
            o_g, lse_g = dilated_attention(q[:, :, g], k[:, :, g], v[:, :, g], dilation, window // dilation)
            outs.append(o_g)
            lses.append(lse_g)
        alpha = jax.nn.softmax(jnp.stack(lses), axis=0)
        att = jnp.sum(alpha[..., None] * jnp.stack(outs), axis=0)
        y_att = att.reshape(bsz, s, ATT_OUT_DIM).astype(dtype) @ w_att_proj[i]

        xbc = lax.conv_general_dilated(
            xbc.astype(jnp.float32), conv_w[i].astype(jnp.float32)[:, None, :],
            window_strides=(1,), padding=[(CONV_WIDTH - 1, 0)],
            dimension_numbers=('NWC', 'WIO', 'NWC'), feature_group_count=XBC_DIM)
        xbc = jax.nn.silu(xbc + conv_b[i].astype(jnp.float32))
        x_ssm, b_mat, c_mat = jnp.split(xbc, (D_INNER, D_INNER + N_SSM_GROUPS * D_STATE), axis=-1)
        x_ssm = x_ssm.reshape(bsz, s, N_SSM_GROUPS, HEADS_PER_SSM_GROUP, SSM_HEAD_DIM)
        b_mat = b_mat.reshape(bsz, s, N_SSM_GROUPS, D_STATE)
        c_mat = c_mat.reshape(bsz, s, N_SSM_GROUPS, D_STATE)
        dt = jax.nn.softplus(dt_raw.astype(jnp.float32) + dt_bias[i].astype(jnp.float32))
        dt = dt.reshape(bsz, s, N_SSM_GROUPS, HEADS_PER_SSM_GROUP)
        a = -jnp.exp(a_log[i].astype(jnp.float32)).reshape(N_SSM_GROUPS, HEADS_PER_SSM_GROUP)
        y = ssd_chunked_scan(x_ssm, dt, a, b_mat, c_mat)
        y = y + d_skip[i].astype(jnp.float32).reshape(N_SSM_GROUPS, HEADS_PER_SSM_GROUP)[..., None] * x_ssm
        y = y.reshape(bsz, s, D_INNER) * jax.nn.silu(z.astype(jnp.float32))
        y = rms_norm_f32(y.reshape(bsz, s, N_SSM_GROUPS, D_INNER // N_SSM_GROUPS),
                         ssm_norm_w[i].reshape(N_SSM_GROUPS, D_INNER // N_SSM_GROUPS))
        y_ssm = y.reshape(bsz, s, D_INNER).astype(dtype) @ w_ssm_proj[i]

        gates = jax.nn.sigmoid(gate_pre.astype(jnp.float32) + b_gates[i].astype(jnp.float32))
        g_att, g_ssm = jnp.split(gates, 2, axis=-1)
        mixed = (g_att * y_att.astype(jnp.float32) + g_ssm * y_ssm.astype(jnp.float32)).astype(dtype)
        x = x + (mixed @ w_out[i]).astype(dtype)

        h = rms_norm_f32(x, ffn2_norm_w[i]).astype(dtype)
        x = x + (0.5 * swiglu(h, ffn2_w_gate[i], ffn2_w_up[i], ffn2_w_down[i])).astype(dtype)
    return x
```

```python
import functools
import math

import jax
import jax.numpy as jnp
from jax import lax
from jax.experimental import pallas as pl
from jax.experimental.pallas import tpu as pltpu

F32 = jnp.float32
BF16 = jnp.bfloat16

NORM_EPS = 1e-6
ROPE_THETA = 10000.0
HEAD_DIM = 64
ATT_DILATIONS = (1, 4, 16)
ATT_BAND = 128
HEADS_PER_GROUP = 8
ATT_GROUP_COLS = HEADS_PER_GROUP * HEAD_DIM
SSM_HEAD_DIM = 64
N_SSM_GROUPS = 4
D_STATE = 128
CONV_WIDTH = 4
SSD_CHUNK = 128
LANES = 128
CONV_CARRY_ROWS = 8
DT_PAD_COLS = 512
PROJ_TN = 512
VMEM_LIMIT = 56 * 1024 * 1024


def _cparams(n_axes):
    return pltpu.CompilerParams(dimension_semantics=("arbitrary",) * n_axes,
                                vmem_limit_bytes=VMEM_LIMIT)


def _rms_norm(x, w):
    return x * lax.rsqrt(jnp.mean(x * x, axis=-1, keepdims=True) + NORM_EPS) * w


def _silu(x):
    return x * jax.nn.sigmoid(x)


def _resident(shape):
    return pl.BlockSpec(shape, lambda *_: (0,) * len(shape), pipeline_mode=pl.Buffered(1))


def _ffn_kernel(x_ref, nw_ref, wg_ref, wu_ref, wd_ref, o_ref):
    x = x_ref[...]
    h = _rms_norm(x, nw_ref[...]).astype(BF16)
    g = jnp.dot(h, wg_ref[...], preferred_element_type=F32)
    u = jnp.dot(h, wu_ref[...], preferred_element_type=F32)
    a = (_silu(g) * u).astype(BF16)
    y = jnp.dot(a, wd_ref[...], preferred_element_type=F32)
    o_ref[...] = x + 0.5 * y


def _ffn(x, nw, wg, wu, wd, tm=512):
    t, d = x.shape
    f = wg.shape[1]
    return pl.pallas_call(
        _ffn_kernel,
        grid=(t // tm,),
        in_specs=[pl.BlockSpec((tm, d), lambda i: (i, 0)),
                  _resident((1, d)), _resident((d, f)), _resident((d, f)), _resident((f, d))],
        out_specs=pl.BlockSpec((tm, d), lambda i: (i, 0)),
        out_shape=jax.ShapeDtypeStruct((t, d), F32),
        compiler_params=_cparams(1),
        name="ffn",
    )(x, nw, wg, wu, wd)


def _proj_kernel(x_ref, nw_ref, w_ref, o_ref, h_scr):
    @pl.when(pl.program_id(1) == 0)
    def _():
        h_scr[...] = _rms_norm(x_ref[...], nw_ref[...]).astype(BF16)

    o_ref[...] = jnp.dot(h_scr[...], w_ref[...], preferred_element_type=F32)


def _proj(x, nw, w, tm=1024, tn=PROJ_TN):
    t, d = x.shape
    n = w.shape[1]
    return pl.pallas_call(
        _proj_kernel,
        grid=(t // tm, n // tn),
        in_specs=[pl.BlockSpec((tm, d), lambda i, j: (i, 0)),
                  pl.BlockSpec((1, d), lambda i, j: (0, 0)),
                  pl.BlockSpec((d, tn), lambda i, j: (0, j))],
        out_specs=pl.BlockSpec((tm, tn), lambda i, j: (i, j)),
        out_shape=jax.ShapeDtypeStruct((t, n), F32),
        scratch_shapes=[pltpu.VMEM((tm, d), BF16)],
        compiler_params=_cparams(2),
        name="proj_zxg",
    )(x, nw, w)


def _qkv_kernel(x_ref, nw_ref, w_ref, hnw_ref, cos_ref, sin_ref, bd_ref,
                q_ref, k_ref, v_ref, h_scr, res_scr, *, dil, tm):
    j = pl.program_id(1)
    n = tm // dil

    @pl.when(j == 0)
    def _():
        h_scr[...] = _rms_norm(x_ref[...], nw_ref[...]).astype(BF16)

    n_chunks = ATT_GROUP_COLS // LANES
    res = jnp.dot(h_scr[...], w_ref[...], preferred_element_type=F32)
    for c in range(n_chunks):
        res_scr[c] = res[:, c * LANES:(c + 1) * LANES]

    def rows(r):
        return pl.ds(r, n, stride=dil) if dil > 1 else pl.ds(0, n)

    def store_classes(out_ref):
        for r in range(dil):
            for c in range(n_chunks):
                out_ref[r, :, c * LANES:(c + 1) * LANES] = res_scr[c, rows(r), :].astype(BF16)

    def norm_rope(out_ref):
        lane = lax.broadcasted_iota(jnp.int32, (tm, LANES), 1)
        first_half = (lane % HEAD_DIM) < (HEAD_DIM // 2)
        for c in range(n_chunks):
            xr = res_scr[c]
            x2 = xr * xr
            hi = x2.astype(BF16)
            lo = (x2 - hi.astype(F32)).astype(BF16)
            ms = (jnp.dot(hi, bd_ref[...], preferred_element_type=F32)
                  + jnp.dot(lo, bd_ref[...], preferred_element_type=F32))
            y = xr * lax.rsqrt(ms + NORM_EPS) * hnw_ref[...]
            partner = jnp.where(first_half,
                                pltpu.roll(y, LANES - HEAD_DIM // 2, 1),
                                pltpu.roll(y, HEAD_DIM // 2, 1))
            res_scr[c] = y * cos_ref[...] + partner * sin_ref[...]
        store_classes(out_ref)

    @pl.when(j == 0)
    def _():
        norm_rope(q_ref)

    @pl.when(j == 1)
    def _():
        norm_rope(k_ref)

    @pl.when(j == 2)
    def _():
        store_classes(v_ref)


def _qkv(x, nw, w, hnw, cos, sin, bd, *, bsz, seq, dil, tm=1024):
    t, d = x.shape
    tiles_per_seq = seq // tm
    n = tm // dil
    gc = ATT_GROUP_COLS
    out_sds = jax.ShapeDtypeStruct((bsz, dil, seq // dil, gc), BF16)
    out_spec = pl.BlockSpec((None, dil, n, gc),
                            lambda i, j: (i // tiles_per_seq, 0, i % tiles_per_seq, 0))
    return pl.pallas_call(
        functools.partial(_qkv_kernel, dil=dil, tm=tm),
        grid=(t // tm, 3),
        in_specs=[pl.BlockSpec((tm, d), lambda i, j: (i, 0)),
                  pl.BlockSpec((1, d), lambda i, j: (0, 0)),
                  pl.BlockSpec((d, gc), lambda i, j: (0, j)),
                  pl.BlockSpec((None, 1, LANES), lambda i, j: (jnp.minimum(j, 1), 0, 0)),
                  pl.BlockSpec((tm, LANES), lambda i, j: (i % tiles_per_seq, 0)),
                  pl.BlockSpec((tm, LANES), lambda i, j: (i % tiles_per_seq, 0)),
                  pl.BlockSpec((LANES, LANES), lambda i, j: (0, 0))],
        out_specs=[out_spec, out_spec, out_spec],
        out_shape=[out_sds, out_sds, out_sds],
        scratch_shapes=[pltpu.VMEM((tm, d), BF16), pltpu.VMEM((gc // LANES, tm, LANES), F32)],
        compiler_params=_cparams(2),
        name=f"qkv_d{dil}",
    )(x, nw, w, hnw, cos, sin, bd)


def _attn_kernel(q0, k0, v0, q1, k1, v1, q2, k2, v2, o_ref, acc_o, acc_m, acc_l, *, seq):
    band = ATT_BAND
    lane = lax.broadcasted_iota(jnp.int32, (band, LANES), 1)
    head0 = lane < HEAD_DIM
    iq = lax.broadcasted_iota(jnp.int32, (band, 2 * band), 0)
    ik = lax.broadcasted_iota(jnp.int32, (band, 2 * band), 1)
    band_mask = (ik >= iq) & (ik <= iq + band)
    causal_mask = (lax.broadcasted_iota(jnp.int32, (band, band), 1)
                   <= lax.broadcasted_iota(jnp.int32, (band, band), 0))

    def block(qb, kk, vv, mask):
        outs = []
        for hsel in (head0, ~head0):
            qh = jnp.where(hsel, qb, jnp.zeros_like(qb))
            s = lax.dot_general(qh, kk, (((1,), (1,)), ((), ())), preferred_element_type=F32)
            s = jnp.where(mask, s, -jnp.inf)
            m = jnp.max(s, axis=-1, keepdims=True)
            p = jnp.exp(s - m)
            l = jnp.sum(p, axis=-1, keepdims=True)
            o = jnp.dot(p.astype(BF16), vv, preferred_element_type=F32)
            outs.append((o, m, l))
        (oa, ma, la), (ob, mb, lb) = outs
        return (jnp.where(head0, oa, ob), jnp.where(head0, ma, mb), jnp.where(head0, la, lb))

    def store(g, dil, r, n, res):
        start = r + dil * band * n
        rows = pl.ds(start, band, stride=dil) if dil > 1 else pl.ds(start, band)
        o, m, l = res
        acc_o[g, rows, :] = o
        acc_m[g, rows, :] = m
        acc_l[g, rows, :] = l

    def run_class(g, dil, q_ref, k_ref, v_ref, r):
        nb = seq // dil // band
        store(g, dil, r, 0, block(q_ref[r, pl.ds(0, band), :], k_ref[r, pl.ds(0, band), :],
                                  v_ref[r, pl.ds(0, band), :], causal_mask))
        if nb > 1:
            def body(n, carry):
                qs = pl.multiple_of(n * band, band)
                ks = pl.multiple_of((n - 1) * band, band)
                store(g, dil, r, n, block(q_ref[r, pl.ds(qs, band), :],
                                          k_ref[r, pl.ds(ks, 2 * band), :],
                                          v_ref[r, pl.ds(ks, 2 * band), :], band_mask))
                return carry
            lax.fori_loop(1, nb, body, 0)

    for g, (dil, q_ref, k_ref, v_ref) in enumerate(((ATT_DILATIONS[0], q0, k0, v0),
                                                    (ATT_DILATIONS[1], q1, k1, v1),
                                                    (ATT_DILATIONS[2], q2, k2, v2))):
        if dil <= 4:
            for r in range(dil):
                run_class(g, dil, q_ref, k_ref, v_ref, r)
        else:
            def class_body(r, carry, g=g, dil=dil, q_ref=q_ref, k_ref=k_ref, v_ref=v_ref):
                run_class(g, dil, q_ref, k_ref, v_ref, r)
                return carry
            lax.fori_loop(0, dil, class_body, 0)

    def merge_rows(i, carry):
        rows = pl.ds(pl.multiple_of(i * band, band), band)
        m0, m1, m2 = acc_m[0, rows, :], acc_m[1, rows, :], acc_m[2, rows, :]
        m = jnp.maximum(jnp.maximum(m0, m1), m2)
        w0, w1, w2 = jnp.exp(m0 - m), jnp.exp(m1 - m), jnp.exp(m2 - m)
        num = w0 * acc_o[0, rows, :] + w1 * acc_o[1, rows, :] + w2 * acc_o[2, rows, :]
        den = w0 * acc_l[0, rows, :] + w1 * acc_l[1, rows, :] + w2 * acc_l[2, rows, :]
        o_ref[rows, :] = (num / den).astype(o_ref.dtype)
        return carry

    lax.fori_loop(0, seq // band, merge_rows, 0)


def _attention(qkv, *, bsz, seq):
    n_pairs = ATT_GROUP_COLS // LANES
    in_specs = []
    for dil in ATT_DILATIONS:
        spec = pl.BlockSpec((None, dil, seq // dil, LANES), lambda b, p: (b, 0, 0, p))
        in_specs += [spec, spec, spec]
    return pl.pallas_call(
        functools.partial(_attn_kernel, seq=seq),
        grid=(bsz, n_pairs),
        in_specs=in_specs,
        out_specs=pl.BlockSpec((None, seq, LANES), lambda b, p: (b, 0, p)),
        out_shape=jax.ShapeDtypeStruct((bsz, seq, ATT_GROUP_COLS), BF16),
        scratch_shapes=[pltpu.VMEM((len(ATT_DILATIONS), seq, LANES), F32)] * 3,
        compiler_params=_cparams(2),
        name="dilated_attn",
    )(*qkv)


def _ssd_kernel(z_ref, x_ref, bc_ref, dt_ref, cwx_ref, cbx_ref, cwbc_ref, cbbc_ref,
                dtb_ref, alog_ref, dskip_ref, nw_ref, o_ref,
                xs_scr, bcs_scr, st_scr, xf_scr, bcf_scr, y_scr, cum_scr, cumt_scr, dtt_scr, swt_scr):
    L = SSD_CHUNK
    hist = CONV_CARRY_ROWS
    n_state = D_STATE
    d_inner = x_ref.shape[-1]
    n_heads = d_inner // SSM_HEAD_DIM
    heads_per_group = n_heads // N_SSM_GROUPS
    group_w = d_inner // N_SSM_GROUPS
    c = pl.program_id(1)

    @pl.when(c == 0)
    def _():
        xs_scr[pl.ds(0, hist), :] = jnp.zeros((hist, xs_scr.shape[1]), F32)
        bcs_scr[pl.ds(0, hist), :] = jnp.zeros((hist, bcs_scr.shape[1]), F32)
        st_scr[...] = jnp.zeros_like(st_scr)

    def conv_silu(src_ref, scr, w_ref, b_ref, dst):
        scr[pl.ds(hist, L), :] = src_ref[...]
        acc = b_ref[...] + w_ref[0:1, :] * scr[pl.ds(hist - CONV_WIDTH + 1, L), :]
        for k in range(1, CONV_WIDTH):
            acc = acc + w_ref[k:k + 1, :] * scr[pl.ds(hist - CONV_WIDTH + 1 + k, L), :]
        dst[...] = _silu(acc)
        scr[pl.ds(0, hist), :] = scr[pl.ds(L, hist), :]

    conv_silu(x_ref, xs_scr, cwx_ref, cbx_ref, xf_scr)
    conv_silu(bc_ref, bcs_scr, cwbc_ref, cbbc_ref, bcf_scr)

    dt_pre = dt_ref[...] + dtb_ref[...]
    dt = jnp.maximum(dt_pre, 0.0) + jnp.log1p(jnp.exp(-jnp.abs(dt_pre)))
    a = -jnp.exp(alog_ref[...])
    ti = lax.broadcasted_iota(jnp.int32, (L, L), 0)
    tj = lax.broadcasted_iota(jnp.int32, (L, L), 1)
    causal = ti >= tj
    cum = jnp.dot(causal.astype(F32), dt * a, preferred_element_type=F32,
                  precision=lax.Precision.HIGHEST)
    cum_t = cum.T
    dt_t = dt.T
    cum_scr[...] = cum
    cumt_scr[...] = cum_t
    dtt_scr[...] = dt_t
    swt_scr[...] = dt_t * jnp.exp(cum_t[:, L - 1:L] - cum_t)

    lane = lax.broadcasted_iota(jnp.int32, (L, LANES), 1)
    head0 = lane < SSM_HEAD_DIM

    for g in range(N_SSM_GROUPS):
        b_g = bcf_scr[:, g * n_state:(g + 1) * n_state]
        c_g = bcf_scr[:, (N_SSM_GROUPS + g) * n_state:(N_SSM_GROUPS + g + 1) * n_state]
        bt_g = b_g.T
        cb = jnp.dot(c_g.astype(BF16), bt_g.astype(BF16), preferred_element_type=F32)
        for pair in range(heads_per_group // 2):
            col = g * group_w + pair * LANES
            cols = slice(col, col + LANES)
            x_pair = xf_scr[:, cols]
            x_pair_b = x_pair.astype(BF16)
            st_pair = st_scr[:, cols]
            rhs = jnp.concatenate([x_pair_b, st_pair.astype(BF16)], axis=0)
            ys, ds, es = [], [], []
            for h in (g * heads_per_group + 2 * pair, g * heads_per_group + 2 * pair + 1):
                colb = jnp.broadcast_to(cum_scr[:, h:h + 1], (L, L))
                rowb = cumt_scr[h:h + 1, :]
                dec = jnp.exp(jnp.where(causal, colb - rowb, -jnp.inf))
                m_h = (cb * dec * dtt_scr[h:h + 1, :]).astype(BF16)
                c_h = (c_g * jnp.exp(colb)).astype(BF16)
                ys.append(jnp.dot(jnp.concatenate([m_h, c_h], axis=1), rhs,
                                  preferred_element_type=F32))
                bt_h = (bt_g * swt_scr[h:h + 1, :]).astype(BF16)
                ds.append(jnp.dot(bt_h, x_pair_b, preferred_element_type=F32))
                es.append(jnp.exp(cumt_scr[h:h + 1, L - 1:L]))
            st_scr[:, cols] = (st_pair * jnp.where(head0, es[0], es[1])
                               + jnp.where(head0, ds[0], ds[1]))
            y_scr[:, cols] = jnp.where(head0, ys[0], ys[1]) + dskip_ref[:, cols] * x_pair

    for g in range(N_SSM_GROUPS):
        cols = slice(g * group_w, (g + 1) * group_w)
        y = y_scr[:, cols] * _silu(z_ref[:, cols])
        o_ref[:, cols] = _rms_norm(y, nw_ref[:, cols]).astype(o_ref.dtype)


def _zxg_offsets(d_inner, d_model):
    bc_w = 2 * N_SSM_GROUPS * D_STATE
    offs = {"z": 0, "x": d_inner, "gates": 2 * d_inner, "bc": 2 * d_inner + 2 * d_model,
            "dt": 2 * d_inner + 2 * d_model + bc_w}
    assert offs["gates"] % (2 * d_model) == 0 and offs["bc"] % bc_w == 0 and offs["dt"] % LANES == 0
    return offs


def _ssd(zxg, cwx, cbx, cwbc, cbbc, dtb, alog, dskip, nw, *, bsz, seq, d_inner, d_model):
    L = SSD_CHUNK
    bc_w = 2 * N_SSM_GROUPS * D_STATE
    zxg3 = zxg.reshape(bsz, seq, zxg.shape[-1])
    offs = _zxg_offsets(d_inner, d_model)
    bc_blk = offs["bc"] // bc_w
    dt_blk = offs["dt"] // LANES

    def small(shape):
        return pl.BlockSpec(shape, lambda b, c: (0,) * len(shape))

    out = pl.pallas_call(
        _ssd_kernel,
        grid=(bsz, seq // L),
        in_specs=[pl.BlockSpec((None, L, d_inner), lambda b, c: (b, c, 0)),
                  pl.BlockSpec((None, L, d_inner), lambda b, c: (b, c, 1)),
                  pl.BlockSpec((None, L, bc_w), lambda b, c: (b, c, bc_blk)),
                  pl.BlockSpec((None, L, LANES), lambda b, c: (b, c, dt_blk)),
                  small((CONV_WIDTH, d_inner)), small((1, d_inner)),
                  small((CONV_WIDTH, bc_w)), small((1, bc_w)),
                  small((1, LANES)), small((1, LANES)), small((1, d_inner)), small((1, d_inner))],
        out_specs=pl.BlockSpec((None, L, d_inner), lambda b, c: (b, c, 0)),
        out_shape=jax.ShapeDtypeStruct((bsz, seq, d_inner), BF16),
        scratch_shapes=[pltpu.VMEM((L + CONV_CARRY_ROWS, d_inner), F32),
                        pltpu.VMEM((L + CONV_CARRY_ROWS, bc_w), F32),
                        pltpu.VMEM((D_STATE, d_inner), F32),
                        pltpu.VMEM((L, d_inner), F32),
                        pltpu.VMEM((L, bc_w), F32),
                        pltpu.VMEM((L, d_inner), F32),
                        pltpu.VMEM((L, LANES), F32),
                        pltpu.VMEM((LANES, L), F32),
                        pltpu.VMEM((LANES, L), F32),
                        pltpu.VMEM((LANES, L), F32)],
        compiler_params=_cparams(2),
        name="ssd",
    )(zxg3, zxg3, zxg3, zxg3, cwx, cbx, cwbc, cbbc, dtb, alog, dskip, nw)
    return out.reshape(bsz * seq, d_inner)


def _merge_kernel(x_ref, att_ref, yn_ref, gp_ref, bg_ref, wa_ref, ws_ref, wo_ref, o_ref):
    d = x_ref.shape[-1]
    ya = jnp.dot(att_ref[...], wa_ref[...], preferred_element_type=F32)
    ys = jnp.dot(yn_ref[...], ws_ref[...], preferred_element_type=F32)
    gates = jax.nn.sigmoid(gp_ref[...] + bg_ref[...])
    mixed = (gates[:, :d] * ya + gates[:, d:] * ys).astype(BF16)
    o_ref[...] = x_ref[...] + jnp.dot(mixed, wo_ref[...], preferred_element_type=F32)


def _merge(x, att, yn, zxg, bg, wa, ws, wo, tm=512):
    t, d = x.shape
    gate_blk = _zxg_offsets(yn.shape[1], d)["gates"] // (2 * d)
    return pl.pallas_call(
        _merge_kernel,
        grid=(t // tm,),
        in_specs=[pl.BlockSpec((tm, d), lambda i: (i, 0)),
                  pl.BlockSpec((tm, att.shape[1]), lambda i: (i, 0)),
                  pl.BlockSpec((tm, yn.shape[1]), lambda i: (i, 0)),
                  pl.BlockSpec((tm, 2 * d), lambda i: (i, gate_blk)),
                  _resident((1, 2 * d)), _resident(wa.shape), _resident(ws.shape), _resident(wo.shape)],
        out_specs=pl.BlockSpec((tm, d), lambda i: (i, 0)),
        out_shape=jax.ShapeDtypeStruct((t, d), F32),
        compiler_params=_cparams(1),
        name="merge",
    )(x, att, yn, zxg, bg, wa, ws, wo)


def _rope_tables(seq):
    pos = jnp.arange(seq, dtype=F32)
    inv_freq = 1.0 / (ROPE_THETA ** (jnp.arange(0, HEAD_DIM, 2, dtype=F32) / HEAD_DIM))
    ang = pos[:, None] * inv_freq[None, :]
    cos, sin = jnp.cos(ang), jnp.sin(ang)
    reps = LANES // HEAD_DIM
    return (jnp.tile(jnp.concatenate([cos, cos], axis=1), (1, reps)),
            jnp.tile(jnp.concatenate([-sin, sin], axis=1), (1, reps)))


def kernel(x, ffn1_norm_w, ffn1_w_gate, ffn1_w_up, ffn1_w_down, mix_norm_w, w_in, b_gates,
           q_norm_w, k_norm_w, conv_w, conv_b, dt_bias, a_log, d_skip, ssm_norm_w,
           w_att_proj, w_ssm_proj, w_out, ffn2_norm_w, ffn2_w_gate, ffn2_w_up, ffn2_w_down):
    bsz, seq, d_model = x.shape
    depth = w_in.shape[0]
    d_inner = ssm_norm_w.shape[1]
    n_ssm_heads = dt_bias.shape[1]
    qkv_cols = len(ATT_DILATIONS) * ATT_GROUP_COLS
    bc_w = 2 * N_SSM_GROUPS * D_STATE
    assert seq % (ATT_DILATIONS[-1] * ATT_BAND) == 0 and seq % 1024 == 0
    assert w_in.shape[2] == 3 * qkv_cols + d_inner + (d_inner + bc_w) + n_ssm_heads + 2 * d_model
    assert 3 * d_inner % bc_w == 0 and n_ssm_heads <= LANES

    o_z = 3 * qkv_cols
    o_x = o_z + d_inner
    o_bc = o_x + d_inner
    o_dt = o_bc + bc_w
    o_g = o_dt + n_ssm_heads
    w_in_b = w_in.astype(BF16)
    gc = ATT_GROUP_COLS
    w_groups = [jnp.concatenate([w_in_b[:, :, s * qkv_cols + g * gc: s * qkv_cols + (g + 1) * gc]
                                 for s in range(3)], axis=2) for g in range(len(ATT_DILATIONS))]
    w_zxg = jnp.concatenate([
        w_in_b[:, :, o_z:o_x], w_in_b[:, :, o_x:o_bc], w_in_b[:, :, o_g:], w_in_b[:, :, o_bc:o_dt],
        jnp.pad(w_in_b[:, :, o_dt:o_g], ((0, 0), (0, 0), (0, DT_PAD_COLS - n_ssm_heads)))], axis=2)
    f1g, f1u, f1d = ffn1_w_gate.astype(BF16), ffn1_w_up.astype(BF16), ffn1_w_down.astype(BF16)
    f2g, f2u, f2d = ffn2_w_gate.astype(BF16), ffn2_w_up.astype(BF16), ffn2_w_down.astype(BF16)
    wa_b, ws_b, wo_b = w_att_proj.astype(BF16), w_ssm_proj.astype(BF16), w_out.astype(BF16)

    reps = LANES // HEAD_DIM
    head_nw = jnp.stack([jnp.tile(q_norm_w * (HEAD_DIM ** -0.5), (1, reps)),
                         jnp.tile(k_norm_w, (1, reps))], axis=1)[:, :, None, :]
    cos_t, sin_t = _rope_tables(seq)
    hid = jnp.arange(LANES) // HEAD_DIM
    block_diag = jnp.where(hid[:, None] == hid[None, :], 1.0 / HEAD_DIM, 0.0).astype(BF16)
    pad_h = LANES - n_ssm_heads
    dtb_p = jnp.pad(dt_bias, ((0, 0), (0, pad_h)))[:, None, :]
    alog_p = jnp.pad(a_log, ((0, 0), (0, pad_h)))[:, None, :]
    dskip_e = jnp.repeat(d_skip, SSM_HEAD_DIM, axis=1)[:, None, :]

    xt = x.reshape(bsz * seq, d_model)
    for i in range(depth):
        xt = _ffn(xt, ffn1_norm_w[i][None], f1g[i], f1u[i], f1d[i])
        nw = mix_norm_w[i][None]
        qkv = []
        for g, dil in enumerate(ATT_DILATIONS):
            qkv += _qkv(xt, nw, w_groups[g][i], head_nw[i], cos_t, sin_t, block_diag,
                        bsz=bsz, seq=seq, dil=dil)
        zxg = _proj(xt, nw, w_zxg[i])
        att = _attention(qkv, bsz=bsz, seq=seq).reshape(bsz * seq, ATT_GROUP_COLS)
        yn = _ssd(zxg, conv_w[i][:, :d_inner], conv_b[i][None, :d_inner],
                  conv_w[i][:, d_inner:], conv_b[i][None, d_inner:],
                  dtb_p[i], alog_p[i], dskip_e[i], ssm_norm_w[i][None],
                  bsz=bsz, seq=seq, d_inner=d_inner, d_model=d_model)
        xt = _merge(xt, att, yn, zxg, b_gates[i][None], wa_b[i], ws_b[i], wo_b[i])
        xt = _ffn(xt, ffn2_norm_w[i][None], f2g[i], f2u[i], f2d[i])
    return xt.reshape(bsz, seq, d_model)
```

```python
import functools
import math

import jax
import jax.numpy as jnp
from jax import lax
from jax.experimental import pallas as pl
from jax.experimental.pallas import tpu as pltpu

F32 = jnp.float32
BF16 = jnp.bfloat16

NORM_EPS = 1e-6
ROPE_THETA = 10000.0
HEAD_DIM = 64
ATT_DILATIONS = (1, 4, 16)
ATT_BAND = 128
ATT_BLOCK_BATCH = 4
HEADS_PER_GROUP = 8
ATT_GROUP_COLS = HEADS_PER_GROUP * HEAD_DIM
SSM_HEAD_DIM = 64
N_SSM_GROUPS = 4
D_STATE = 128
CONV_WIDTH = 4
SSD_CHUNK = 128
LANES = 128
CONV_CARRY_ROWS = 8
DT_PAD_COLS = 512
PROJ_TN = 1536
VMEM_LIMIT = 56 * 1024 * 1024


def _cparams(n_axes):
    return pltpu.CompilerParams(dimension_semantics=("arbitrary",) * n_axes,
                                vmem_limit_bytes=VMEM_LIMIT)


def _rms_norm(x, w):
    return x * lax.rsqrt(jnp.mean(x * x, axis=-1, keepdims=True) + NORM_EPS) * w


def _silu(x):
    return x * jax.nn.sigmoid(x)


def _resident(shape):
    return pl.BlockSpec(shape, lambda *_: (0,) * len(shape), pipeline_mode=pl.Buffered(1))


def _ffn_kernel(x_ref, nw_ref, wg_ref, wu_ref, wd_ref, o_ref):
    x = x_ref[...]
    h = _rms_norm(x, nw_ref[...]).astype(BF16)
    g = jnp.dot(h, wg_ref[...], preferred_element_type=F32)
    u = jnp.dot(h, wu_ref[...], preferred_element_type=F32)
    a = (_silu(g) * u).astype(BF16)
    y = jnp.dot(a, wd_ref[...], preferred_element_type=F32)
    o_ref[...] = x + 0.5 * y


def _ffn(x, nw, wg, wu, wd, tm=512):
    t, d = x.shape
    f = wg.shape[1]
    return pl.pallas_call(
        _ffn_kernel,
        grid=(t // tm,),
        in_specs=[pl.BlockSpec((tm, d), lambda i: (i, 0)),
                  _resident((1, d)), _resident((d, f)), _resident((d, f)), _resident((f, d))],
        out_specs=pl.BlockSpec((tm, d), lambda i: (i, 0)),
        out_shape=jax.ShapeDtypeStruct((t, d), F32),
        compiler_params=_cparams(1),
        name="ffn",
    )(x, nw, wg, wu, wd)


def _proj_kernel(x_ref, nw_ref, w_ref, o_ref, h_scr):
    @pl.when(pl.program_id(1) == 0)
    def _():
        h_scr[...] = _rms_norm(x_ref[...], nw_ref[...]).astype(BF16)

    o_ref[...] = jnp.dot(h_scr[...], w_ref[...], preferred_element_type=F32)


def _proj(x, nw, w, tm=1024, tn=PROJ_TN):
    t, d = x.shape
    n = w.shape[1]
    return pl.pallas_call(
        _proj_kernel,
        grid=(t // tm, n // tn),
        in_specs=[pl.BlockSpec((tm, d), lambda i, j: (i, 0)),
                  pl.BlockSpec((1, d), lambda i, j: (0, 0)),
                  pl.BlockSpec((d, tn), lambda i, j: (0, j))],
        out_specs=pl.BlockSpec((tm, tn), lambda i, j: (i, j)),
        out_shape=jax.ShapeDtypeStruct((t, n), F32),
        scratch_shapes=[pltpu.VMEM((tm, d), BF16)],
        compiler_params=_cparams(2),
        name="proj_zxg",
    )(x, nw, w)


def _qkv_kernel(x_ref, nw_ref, w_ref, hnw_ref, cos_ref, sin_ref, bd_ref,
                q_ref, k_ref, v_ref, h_scr, res_scr, *, dil, tm):
    j = pl.program_id(1)
    n = tm // dil

    @pl.when(j == 0)
    def _():
        h_scr[...] = _rms_norm(x_ref[...], nw_ref[...]).astype(BF16)

    n_chunks = ATT_GROUP_COLS // LANES
    res = jnp.dot(h_scr[...], w_ref[...], preferred_element_type=F32)
    for c in range(n_chunks):
        res_scr[c] = res[:, c * LANES:(c + 1) * LANES]

    def rows(r):
        return pl.ds(r, n, stride=dil) if dil > 1 else pl.ds(0, n)

    def store_classes(out_ref):
        for r in range(dil):
            for c in range(n_chunks):
                out_ref[r, :, c * LANES:(c + 1) * LANES] = res_scr[c, rows(r), :].astype(BF16)

    def norm_rope(out_ref):
        lane = lax.broadcasted_iota(jnp.int32, (tm, LANES), 1)
        first_half = (lane % HEAD_DIM) < (HEAD_DIM // 2)
        for c in range(n_chunks):
            xr = res_scr[c]
            x2 = xr * xr
            hi = x2.astype(BF16)
            lo = (x2 - hi.astype(F32)).astype(BF16)
            ms = (jnp.dot(hi, bd_ref[...], preferred_element_type=F32)
                  + jnp.dot(lo, bd_ref[...], preferred_element_type=F32))
            y = xr * lax.rsqrt(ms + NORM_EPS) * hnw_ref[...]
            partner = jnp.where(first_half,
                                pltpu.roll(y, LANES - HEAD_DIM // 2, 1),
                                pltpu.roll(y, HEAD_DIM // 2, 1))
            res_scr[c] = y * cos_ref[...] + partner * sin_ref[...]
        store_classes(out_ref)

    @pl.when(j == 0)
    def _():
        norm_rope(q_ref)

    @pl.when(j == 1)
    def _():
        norm_rope(k_ref)

    @pl.when(j == 2)
    def _():
        store_classes(v_ref)


def _qkv(x, nw, w, hnw, cos, sin, bd, *, bsz, seq, dil, tm=1024):
    t, d = x.shape
    tiles_per_seq = seq // tm
    n = tm // dil
    gc = ATT_GROUP_COLS
    out_sds = jax.ShapeDtypeStruct((bsz, dil, seq // dil, gc), BF16)
    out_spec = pl.BlockSpec((None, dil, n, gc),
                            lambda i, j: (i // tiles_per_seq, 0, i % tiles_per_seq, 0))
    return pl.pallas_call(
        functools.partial(_qkv_kernel, dil=dil, tm=tm),
        grid=(t // tm, 3),
        in_specs=[pl.BlockSpec((tm, d), lambda i, j: (i, 0)),
                  pl.BlockSpec((1, d), lambda i, j: (0, 0)),
                  pl.BlockSpec((d, gc), lambda i, j: (0, j)),
                  pl.BlockSpec((None, 1, LANES), lambda i, j: (jnp.minimum(j, 1), 0, 0)),
                  pl.BlockSpec((tm, LANES), lambda i, j: (i % tiles_per_seq, 0)),
                  pl.BlockSpec((tm, LANES), lambda i, j: (i % tiles_per_seq, 0)),
                  pl.BlockSpec((LANES, LANES), lambda i, j: (0, 0))],
        out_specs=[out_spec, out_spec, out_spec],
        out_shape=[out_sds, out_sds, out_sds],
        scratch_shapes=[pltpu.VMEM((tm, d), BF16), pltpu.VMEM((gc // LANES, tm, LANES), F32)],
        compiler_params=_cparams(2),
        name=f"qkv_d{dil}",
    )(x, nw, w, hnw, cos, sin, bd)


def _attn_kernel(q0, k0, v0, q1, k1, v1, q2, k2, v2, o_ref, acc_o, acc_m, acc_l, *, seq):
    band = ATT_BAND
    nbatch = ATT_BLOCK_BATCH
    lane = lax.broadcasted_iota(jnp.int32, (band, LANES), 1)
    head0 = lane < HEAD_DIM
    iq = lax.broadcasted_iota(jnp.int32, (band, band), 0)
    ik = lax.broadcasted_iota(jnp.int32, (band, band), 1)
    cur_mask = ik <= iq
    prev_mask = ik >= iq
    nt_dims = (((2,), (2,)), ((0,), (0,)))
    nn_dims = (((2,), (1,)), ((0,), (0,)))

    ones = jnp.ones((nbatch, band, LANES), BF16)
    cur_mask2 = jnp.concatenate([cur_mask, cur_mask], axis=0)
    prev_mask2 = jnp.concatenate([prev_mask, prev_mask], axis=0)

    def blocks(qb, kc, vc, kp=None, vp=None, first_has_no_prev=False):
        zero = jnp.zeros_like(qb)
        qs = jnp.concatenate([jnp.where(head0, qb, zero), jnp.where(head0, zero, qb)], axis=1)
        sc = lax.dot_general(qs, kc, nt_dims, preferred_element_type=F32)
        sc = jnp.where(cur_mask2, sc, -jnp.inf)
        m = jnp.max(sc, axis=-1, keepdims=True)
        if kp is not None:
            sp = lax.dot_general(qs, kp, nt_dims, preferred_element_type=F32)
            sp = jnp.where(prev_mask2, sp, -jnp.inf)
            if first_has_no_prev:
                bidx = lax.broadcasted_iota(jnp.int32, sp.shape, 0)
                sp = jnp.where(bidx > 0, sp, -jnp.inf)
            m = jnp.maximum(m, jnp.max(sp, axis=-1, keepdims=True))
        acc = lax.dot_general(jnp.exp(sc - m).astype(BF16), jnp.concatenate([vc, ones], axis=-1),
                              nn_dims, preferred_element_type=F32)
        if kp is not None:
            acc = acc + lax.dot_general(jnp.exp(sp - m).astype(BF16),
                                        jnp.concatenate([vp, ones], axis=-1),
                                        nn_dims, preferred_element_type=F32)
        top, bot = acc[:, :band], acc[:, band:]
        return (jnp.where(head0, top[..., :LANES], bot[..., :LANES]),
                jnp.where(head0, m[:, :band], m[:, band:]),
                jnp.where(head0, top[..., LANES:], bot[..., LANES:]))

    def store(g, rows, res, b=None):
        for acc, val in zip((acc_o, acc_m, acc_l), res):
            acc[g, rows, :] = val.reshape(nbatch * band, LANES) if b is None else val[b]

    span = nbatch * band
    for n0 in range(0, seq // band, nbatch):
        def view(ref, start):
            return ref[0, pl.ds(start, span), :].reshape(nbatch, band, LANES)
        if n0 == 0:
            def shifted(ref):
                return jnp.concatenate([ref[0, pl.ds(0, band), :], ref[0, pl.ds(0, span - band), :]],
                                       axis=0).reshape(nbatch, band, LANES)
            kp, vp = shifted(k0), shifted(v0)
        else:
            kp, vp = view(k0, (n0 - 1) * band), view(v0, (n0 - 1) * band)
        res = blocks(view(q0, n0 * band), view(k0, n0 * band), view(v0, n0 * band), kp, vp,
                     first_has_no_prev=(n0 == 0))
        store(0, pl.ds(n0 * band, span), res)

    dil = ATT_DILATIONS[1]
    assert dil == nbatch
    for n in range(seq // dil // band):
        cur = pl.ds(n * band, band)
        if n == 0:
            res = blocks(q1[:, cur, :], k1[:, cur, :], v1[:, cur, :])
        else:
            prev = pl.ds((n - 1) * band, band)
            res = blocks(q1[:, cur, :], k1[:, cur, :], v1[:, cur, :], k1[:, prev, :], v1[:, prev, :])
        for r in range(dil):
            store(1, pl.ds(r + dil * band * n, band, stride=dil), res, b=r)

    dil = ATT_DILATIONS[2]
    assert seq // dil == band
    for r0 in range(0, dil, nbatch):
        cls = pl.ds(r0, nbatch)
        res = blocks(q2[cls], k2[cls], v2[cls])
        for b in range(nbatch):
            store(2, pl.ds(r0 + b, band, stride=dil), res, b=b)

    def merge_rows(i, carry):
        rows = pl.ds(pl.multiple_of(i * band, band), band)
        m0, m1, m2 = acc_m[0, rows, :], acc_m[1, rows, :], acc_m[2, rows, :]
        m = jnp.maximum(jnp.maximum(m0, m1), m2)
        w0, w1, w2 = jnp.exp(m0 - m), jnp.exp(m1 - m), jnp.exp(m2 - m)
        num = w0 * acc_o[0, rows, :] + w1 * acc_o[1, rows, :] + w2 * acc_o[2, rows, :]
        den = w0 * acc_l[0, rows, :] + w1 * acc_l[1, rows, :] + w2 * acc_l[2, rows, :]
        o_ref[rows, :] = (num / den).astype(o_ref.dtype)
        return carry

    lax.fori_loop(0, seq // band, merge_rows, 0)


def _attention(qkv, *, bsz, seq):
    n_pairs = ATT_GROUP_COLS // LANES
    in_specs = []
    for dil in ATT_DILATIONS:
        spec = pl.BlockSpec((None, dil, seq // dil, LANES), lambda b, p: (b, 0, 0, p))
        in_specs += [spec, spec, spec]
    return pl.pallas_call(
        functools.partial(_attn_kernel, seq=seq),
        grid=(bsz, n_pairs),
        in_specs=in_specs,
        out_specs=pl.BlockSpec((None, seq, LANES), lambda b, p: (b, 0, p)),
        out_shape=jax.ShapeDtypeStruct((bsz, seq, ATT_GROUP_COLS), BF16),
        scratch_shapes=[pltpu.VMEM((len(ATT_DILATIONS), seq, LANES), F32)] * 3,
        compiler_params=_cparams(2),
        name="dilated_attn",
    )(*qkv)


def _ssd_kernel(z_ref, x_ref, bc_ref, dt_ref, cwx_ref, cbx_ref, cwbc_ref, cbbc_ref,
                dtb_ref, alog_ref, dskip_ref, nw_ref, o_ref,
                xs_scr, bcs_scr, st_scr, xf_scr, bcf_scr, y_scr, cum_scr, cumt_scr, dtt_scr, swt_scr):
    L = SSD_CHUNK
    hist = CONV_CARRY_ROWS
    n_state = D_STATE
    d_inner = x_ref.shape[-1]
    n_heads = d_inner // SSM_HEAD_DIM
    heads_per_group = n_heads // N_SSM_GROUPS
    group_w = d_inner // N_SSM_GROUPS
    c = pl.program_id(1)

    @pl.when(c == 0)
    def _():
        xs_scr[pl.ds(0, hist), :] = jnp.zeros((hist, xs_scr.shape[1]), F32)
        bcs_scr[pl.ds(0, hist), :] = jnp.zeros((hist, bcs_scr.shape[1]), F32)
        st_scr[...] = jnp.zeros_like(st_scr)

    def conv_silu(src_ref, scr, w_ref, b_ref, dst):
        scr[pl.ds(hist, L), :] = src_ref[...]
        acc = b_ref[...] + w_ref[0:1, :] * scr[pl.ds(hist - CONV_WIDTH + 1, L), :]
        for k in range(1, CONV_WIDTH):
            acc = acc + w_ref[k:k + 1, :] * scr[pl.ds(hist - CONV_WIDTH + 1 + k, L), :]
        dst[...] = _silu(acc)
        scr[pl.ds(0, hist), :] = scr[pl.ds(L, hist), :]

    conv_silu(x_ref, xs_scr, cwx_ref, cbx_ref, xf_scr)
    conv_silu(bc_ref, bcs_scr, cwbc_ref, cbbc_ref, bcf_scr)

    dt_pre = dt_ref[...] + dtb_ref[...]
    dt = jnp.maximum(dt_pre, 0.0) + jnp.log1p(jnp.exp(-jnp.abs(dt_pre)))
    a = -jnp.exp(alog_ref[...])
    ti = lax.broadcasted_iota(jnp.int32, (L, L), 0)
    tj = lax.broadcasted_iota(jnp.int32, (L, L), 1)
    causal = ti >= tj
    cum = jnp.dot(causal.astype(F32), dt * a, preferred_element_type=F32,
                  precision=lax.Precision.HIGHEST)
    cum_t = cum.T
    dt_t = dt.T
    cum_scr[...] = cum
    cumt_scr[...] = cum_t
    dtt_scr[...] = dt_t
    swt_scr[...] = dt_t * jnp.exp(cum_t[:, L - 1:L] - cum_t)

    lane = lax.broadcasted_iota(jnp.int32, (L, LANES), 1)
    head0 = lane < SSM_HEAD_DIM

    for g in range(N_SSM_GROUPS):
        b_g = bcf_scr[:, g * n_state:(g + 1) * n_state]
        c_g = bcf_scr[:, (N_SSM_GROUPS + g) * n_state:(N_SSM_GROUPS + g + 1) * n_state]
        bt_g = b_g.T
        cb = jnp.dot(c_g.astype(BF16), bt_g.astype(BF16), preferred_element_type=F32)
        for pair in range(heads_per_group // 2):
            col = g * group_w + pair * LANES
            cols = slice(col, col + LANES)
            x_pair = xf_scr[:, cols]
            x_pair_b = x_pair.astype(BF16)
            st_pair = st_scr[:, cols]
            rhs = jnp.concatenate([x_pair_b, st_pair.astype(BF16)], axis=0)
            ys, ds, es = [], [], []
            for h in (g * heads_per_group + 2 * pair, g * heads_per_group + 2 * pair + 1):
                colb = jnp.broadcast_to(cum_scr[:, h:h + 1], (L, L))
                rowb = cumt_scr[h:h + 1, :]
                dec = jnp.exp(jnp.where(causal, colb - rowb, -jnp.inf))
                m_h = (cb * dec * dtt_scr[h:h + 1, :]).astype(BF16)
                c_h = (c_g * jnp.exp(colb)).astype(BF16)
                ys.append(jnp.dot(jnp.concatenate([m_h, c_h], axis=1), rhs,
                                  preferred_element_type=F32))
                bt_h = (bt_g * swt_scr[h:h + 1, :]).astype(BF16)
                ds.append(jnp.dot(bt_h, x_pair_b, preferred_element_type=F32))
                es.append(jnp.exp(cumt_scr[h:h + 1, L - 1:L]))
            st_scr[:, cols] = (st_pair * jnp.where(head0, es[0], es[1])
                               + jnp.where(head0, ds[0], ds[1]))
            y_scr[:, cols] = jnp.where(head0, ys[0], ys[1]) + dskip_ref[:, cols] * x_pair

    for g in range(N_SSM_GROUPS):
        cols = slice(g * group_w, (g + 1) * group_w)
        y = y_scr[:, cols] * _silu(z_ref[:, cols])
        o_ref[:, cols] = _rms_norm(y, nw_ref[:, cols]).astype(o_ref.dtype)


def _zxg_offsets(d_inner, d_model):
    bc_w = 2 * N_SSM_GROUPS * D_STATE
    offs = {"z": 0, "x": d_inner, "gates": 2 * d_inner, "bc": 2 * d_inner + 2 * d_model,
            "dt": 2 * d_inner + 2 * d_model + bc_w}
    assert offs["gates"] % (2 * d_model) == 0 and offs["bc"] % bc_w == 0 and offs["dt"] % LANES == 0
    return offs


def _ssd(zxg, cwx, cbx, cwbc, cbbc, dtb, alog, dskip, nw, *, bsz, seq, d_inner, d_model):
    L = SSD_CHUNK
    bc_w = 2 * N_SSM_GROUPS * D_STATE
    zxg3 = zxg.reshape(bsz, seq, zxg.shape[-1])
    offs = _zxg_offsets(d_inner, d_model)
    bc_blk = offs["bc"] // bc_w
    dt_blk = offs["dt"] // LANES

    def small(shape):
        return pl.BlockSpec(shape, lambda b, c: (0,) * len(shape))

    out = pl.pallas_call(
        _ssd_kernel,
        grid=(bsz, seq // L),
        in_specs=[pl.BlockSpec((None, L, d_inner), lambda b, c: (b, c, 0)),
                  pl.BlockSpec((None, L, d_inner), lambda b, c: (b, c, 1)),
                  pl.BlockSpec((None, L, bc_w), lambda b, c: (b, c, bc_blk)),
                  pl.BlockSpec((None, L, LANES), lambda b, c: (b, c, dt_blk)),
                  small((CONV_WIDTH, d_inner)), small((1, d_inner)),
                  small((CONV_WIDTH, bc_w)), small((1, bc_w)),
                  small((1, LANES)), small((1, LANES)), small((1, d_inner)), small((1, d_inner))],
        out_specs=pl.BlockSpec((None, L, d_inner), lambda b, c: (b, c, 0)),
        out_shape=jax.ShapeDtypeStruct((bsz, seq, d_inner), BF16),
        scratch_shapes=[pltpu.VMEM((L + CONV_CARRY_ROWS, d_inner), F32),
                        pltpu.VMEM((L + CONV_CARRY_ROWS, bc_w), F32),
                        pltpu.VMEM((D_STATE, d_inner), F32),
                        pltpu.VMEM((L, d_inner), F32),
                        pltpu.VMEM((L, bc_w), F32),
                        pltpu.VMEM((L, d_inner), F32),
                        pltpu.VMEM((L, LANES), F32),
                        pltpu.VMEM((LANES, L), F32),
                        pltpu.VMEM((LANES, L), F32),
                        pltpu.VMEM((LANES, L), F32)],
        compiler_params=_cparams(2),
        name="ssd",
    )(zxg3, zxg3, zxg3, zxg3, cwx, cbx, cwbc, cbbc, dtb, alog, dskip, nw)
    return out.reshape(bsz * seq, d_inner)


def _merge_kernel(x_ref, att_ref, yn_ref, gp_ref, bg_ref, wa_ref, ws_ref, wo_ref, o_ref):
    d = x_ref.shape[-1]
    ya = jnp.dot(att_ref[...], wa_ref[...], preferred_element_type=F32)
    ys = jnp.dot(yn_ref[...], ws_ref[...], preferred_element_type=F32)
    gates = jax.nn.sigmoid(gp_ref[...] + bg_ref[...])
    mixed = (gates[:, :d] * ya + gates[:, d:] * ys).astype(BF16)
    o_ref[...] = x_ref[...] + jnp.dot(mixed, wo_ref[...], preferred_element_type=F32)


def _merge(x, att, yn, zxg, bg, wa, ws, wo, tm=512):
    t, d = x.shape
    gate_blk = _zxg_offsets(yn.shape[1], d)["gates"] // (2 * d)
    return pl.pallas_call(
        _merge_kernel,
        grid=(t // tm,),
        in_specs=[pl.BlockSpec((tm, d), lambda i: (i, 0)),
                  pl.BlockSpec((tm, att.shape[1]), lambda i: (i, 0)),
                  pl.BlockSpec((tm, yn.shape[1]), lambda i: (i, 0)),
                  pl.BlockSpec((tm, 2 * d), lambda i: (i, gate_blk)),
                  _resident((1, 2 * d)), _resident(wa.shape), _resident(ws.shape), _resident(wo.shape)],
        out_specs=pl.BlockSpec((tm, d), lambda i: (i, 0)),
        out_shape=jax.ShapeDtypeStruct((t, d), F32),
        compiler_params=_cparams(1),
        name="merge",
    )(x, att, yn, zxg, bg, wa, ws, wo)


def _rope_tables(seq):
    pos = jnp.arange(seq, dtype=F32)
    inv_freq = 1.0 / (ROPE_THETA ** (jnp.arange(0, HEAD_DIM, 2, dtype=F32) / HEAD_DIM))
    ang = pos[:, None] * inv_freq[None, :]
    cos, sin = jnp.cos(ang), jnp.sin(ang)
    reps = LANES // HEAD_DIM
    return (jnp.tile(jnp.concatenate([cos, cos], axis=1), (1, reps)),
            jnp.tile(jnp.concatenate([-sin, sin], axis=1), (1, reps)))


def kernel(x, ffn1_norm_w, ffn1_w_gate, ffn1_w_up, ffn1_w_down, mix_norm_w, w_in, b_gates,
           q_norm_w, k_norm_w, conv_w, conv_b, dt_bias, a_log, d_skip, ssm_norm_w,
           w_att_proj, w_ssm_proj, w_out, ffn2_norm_w, ffn2_w_gate, ffn2_w_up, ffn2_w_down):
    bsz, seq, d_model = x.shape
    depth = w_in.shape[0]
    d_inner = ssm_norm_w.shape[1]
    n_ssm_heads = dt_bias.shape[1]
    qkv_cols = len(ATT_DILATIONS) * ATT_GROUP_COLS
    bc_w = 2 * N_SSM_GROUPS * D_STATE
    assert seq % (ATT_DILATIONS[-1] * ATT_BAND) == 0 and seq % 1024 == 0
    assert w_in.shape[2] == 3 * qkv_cols + d_inner + (d_inner + bc_w) + n_ssm_heads + 2 * d_model
    assert 3 * d_inner % bc_w == 0 and n_ssm_heads <= LANES

    o_z = 3 * qkv_cols
    o_x = o_z + d_inner
    o_bc = o_x + d_inner
    o_dt = o_bc + bc_w
    o_g = o_dt + n_ssm_heads
    w_in_b = w_in.astype(BF16)
    gc = ATT_GROUP_COLS
    w_groups = [jnp.concatenate([w_in_b[:, :, s * qkv_cols + g * gc: s * qkv_cols + (g + 1) * gc]
                                 for s in range(3)], axis=2) for g in range(len(ATT_DILATIONS))]
    w_zxg = jnp.concatenate([
        w_in_b[:, :, o_z:o_x], w_in_b[:, :, o_x:o_bc], w_in_b[:, :, o_g:], w_in_b[:, :, o_bc:o_dt],
        jnp.pad(w_in_b[:, :, o_dt:o_g], ((0, 0), (0, 0), (0, DT_PAD_COLS - n_ssm_heads)))], axis=2)
    f1g, f1u, f1d = ffn1_w_gate.astype(BF16), ffn1_w_up.astype(BF16), ffn1_w_down.astype(BF16)
    f2g, f2u, f2d = ffn2_w_gate.astype(BF16), ffn2_w_up.astype(BF16), ffn2_w_down.astype(BF16)
    wa_b, ws_b, wo_b = w_att_proj.astype(BF16), w_ssm_proj.astype(BF16), w_out.astype(BF16)

    reps = LANES // HEAD_DIM
    head_nw = jnp.stack([jnp.tile(q_norm_w * (HEAD_DIM ** -0.5), (1, reps)),
                         jnp.tile(k_norm_w, (1, reps))], axis=1)[:, :, None, :]
    cos_t, sin_t = _rope_tables(seq)
    hid = jnp.arange(LANES) // HEAD_DIM
    block_diag = jnp.where(hid[:, None] == hid[None, :], 1.0 / HEAD_DIM, 0.0).astype(BF16)
    pad_h = LANES - n_ssm_heads
    dtb_p = jnp.pad(dt_bias, ((0, 0), (0, pad_h)))[:, None, :]
    alog_p = jnp.pad(a_log, ((0, 0), (0, pad_h)))[:, None, :]
    dskip_e = jnp.repeat(d_skip, SSM_HEAD_DIM, axis=1)[:, None, :]

    xt = x.reshape(bsz * seq, d_model)
    for i in range(depth):
        xt = _ffn(xt, ffn1_norm_w[i][None], f1g[i], f1u[i], f1d[i])
        nw = mix_norm_w[i][None]
        qkv = []
        for g, dil in enumerate(ATT_DILATIONS):
            qkv += _qkv(xt, nw, w_groups[g][i], head_nw[i], cos_t, sin_t, block_diag,
                        bsz=bsz, seq=seq, dil=dil)
        zxg = _proj(xt, nw, w_zxg[i])
        att = _attention(qkv, bsz=bsz, seq=seq).reshape(bsz * seq, ATT_GROUP_COLS)
        yn = _ssd(zxg, conv_w[i][:, :d_inner], conv_b[i][None, :d_inner],
                  conv_w[i][:, d_inner:], conv_b[i][None, d_inner:],
                  dtb_p[i], alog_p[i], dskip_e[i], ssm_norm_w[i][None],
                  bsz=bsz, seq=seq, d_inner=d_inner, d_model=d_model)
        xt = _merge(xt, att, yn, zxg, b_gates[i][None], wa_b[i], ws_b[i], wo_b[i])
        xt = _ffn(xt, ffn2_norm_w[i][None], f2g[i], f2u[i], f2d[i])
    return xt.reshape(bsz, seq, d_model)
```

```python
import functools

import jax
import jax.numpy as jnp
from jax import lax
from jax.experimental import pallas as pl
from jax.experimental.pallas import tpu as pltpu

F32 = jnp.float32
BF16 = jnp.bfloat16

NORM_EPS = 1e-6
ROPE_THETA = 10000.0
HEAD_DIM = 64
ATT_DILATIONS = (1, 4, 16)
ATT_BAND = 128
DEINTERLEAVE_STRIDE = 4
ATT_BLOCK_BATCH = 4
HEADS_PER_GROUP = 8
ATT_GROUP_COLS = HEADS_PER_GROUP * HEAD_DIM
SSM_HEAD_DIM = 64
N_SSM_GROUPS = 4
D_STATE = 128
CONV_WIDTH = 4
SSD_CHUNK = 128
LANES = 128
CONV_CARRY_ROWS = 8
PROJ_CONV_ROWS = 256
PROJ_TN = 1024
VMEM_LIMIT = 56 * 1024 * 1024


def _cparams(n_axes):
    return pltpu.CompilerParams(dimension_semantics=("arbitrary",) * n_axes,
                                vmem_limit_bytes=VMEM_LIMIT)


def _rms_norm(x, w):
    return x * lax.rsqrt(jnp.mean(x * x, axis=-1, keepdims=True) + NORM_EPS) * w


def _sigmoid(x):
    return 0.5 * jnp.tanh(0.5 * x) + 0.5


def _silu(x):
    h = 0.5 * x
    return h * jnp.tanh(h) + h


def _softplus(x):
    return jnp.maximum(x, 0.0) + jnp.log1p(jnp.exp(-jnp.abs(x)))


def _resident(shape):
    return pl.BlockSpec(shape, lambda *_: (0,) * len(shape), pipeline_mode=pl.Buffered(1))


def _ffn_kernel(x_ref, nw_ref, wg_ref, wu_ref, wd_ref, *rest):
    x = x_ref[...]
    h = _rms_norm(x, nw_ref[...]).astype(BF16)
    g = jnp.dot(h, wg_ref[...], preferred_element_type=F32)
    u = jnp.dot(h, wu_ref[...], preferred_element_type=F32)
    a = (_silu(g) * u).astype(BF16)
    y = jnp.dot(a, wd_ref[...], preferred_element_type=F32)
    out = x + 0.5 * y
    if len(rest) == 1:
        rest[0][...] = out
    else:
        nw_next_ref, o_ref, h_ref = rest
        o_ref[...] = out
        h_ref[...] = _rms_norm(out, nw_next_ref[...]).astype(BF16)


def _ffn(x, nw, wg, wu, wd, nw_next=None, tm=512):
    t, d = x.shape
    f = wg.shape[1]
    row = pl.BlockSpec((tm, d), lambda i: (i, 0))
    in_specs = [row, _resident((1, d)), _resident((d, f)), _resident((d, f)), _resident((f, d))]
    args = [x, nw, wg, wu, wd]
    out_specs, out_shape = row, jax.ShapeDtypeStruct((t, d), F32)
    if nw_next is not None:
        in_specs.append(_resident((1, d)))
        args.append(nw_next)
        out_specs, out_shape = [row, row], [out_shape, jax.ShapeDtypeStruct((t, d), BF16)]
    return pl.pallas_call(
        _ffn_kernel,
        grid=(t // tm,),
        in_specs=in_specs,
        out_specs=out_specs,
        out_shape=out_shape,
        compiler_params=_cparams(1),
        name="ffn",
    )(*args)


def _proj_kernel(h_ref, w_ref, wdt_ref, cw_ref, cb_ref, dtb_ref, zxg_ref, bc_ref, dt_ref,
                 conv_scr, hist_scr, *, tm, tiles_per_seq, n_z, n_x, n_g):
    i = pl.program_id(0)
    j = pl.program_id(1)
    hist = CONV_CARRY_ROWS
    sub = PROJ_CONV_ROWS

    def matmul(rows=None):
        lhs = h_ref[...] if rows is None else h_ref[rows, :]
        return jnp.dot(lhs, w_ref[...], preferred_element_type=F32)

    def conv_silu(slot, out_ref):
        @pl.when(i % tiles_per_seq == 0)
        def _():
            conv_scr[pl.ds(0, hist), :] = jnp.zeros((hist, conv_scr.shape[1]), F32)

        @pl.when(i % tiles_per_seq != 0)
        def _():
            conv_scr[pl.ds(0, hist), :] = hist_scr[slot]

        for s in range(tm // sub):
            conv_scr[pl.ds(hist + s * sub, sub), :] = matmul(pl.ds(s * sub, sub))
            base = hist - CONV_WIDTH + 1 + s * sub
            acc = cb_ref[...] + cw_ref[0:1, :] * conv_scr[pl.ds(base, sub), :]
            for k in range(1, CONV_WIDTH):
                acc = acc + cw_ref[k:k + 1, :] * conv_scr[pl.ds(base + k, sub), :]
            out_ref[pl.ds(s * sub, sub), :] = _silu(acc).astype(out_ref.dtype)
        hist_scr[slot] = conv_scr[pl.ds(tm, hist), :]

    for jj in range(n_z + n_x + n_g + 1):
        @pl.when(j == jj)
        def _(jj=jj):
            if jj < n_z:
                zxg_ref[...] = _silu(matmul())
            elif jj < n_z + n_x:
                conv_silu(jj - n_z, zxg_ref)
            elif jj < n_z + n_x + n_g:
                zxg_ref[...] = matmul()
            else:
                conv_silu(n_x, bc_ref)
                dt_raw = jnp.dot(h_ref[...], wdt_ref[...], preferred_element_type=F32)
                dt_ref[...] = _softplus(dt_raw + dtb_ref[...])


def _proj(h, w_main, w_dt, conv_w, conv_b, dt_bias, *, seq, d_inner, d_model, tm=1024, tn=PROJ_TN):
    t, d = h.shape
    bc_w = 2 * N_SSM_GROUPS * D_STATE
    assert bc_w == tn and d_inner % tn == 0 and (2 * d_model) % tn == 0
    n_z, n_x, n_g = d_inner // tn, d_inner // tn, 2 * d_model // tn
    n_f32 = n_z + n_x + n_g
    return pl.pallas_call(
        functools.partial(_proj_kernel, tm=tm, tiles_per_seq=seq // tm, n_z=n_z, n_x=n_x, n_g=n_g),
        grid=(t // tm, n_f32 + 1),
        in_specs=[pl.BlockSpec((tm, d), lambda i, j: (i, 0)),
                  pl.BlockSpec((d, tn), lambda i, j: (0, j)),
                  pl.BlockSpec((d, LANES), lambda i, j: (0, 0)),
                  pl.BlockSpec((CONV_WIDTH, tn), lambda i, j: (0, j)),
                  pl.BlockSpec((1, tn), lambda i, j: (0, j)),
                  pl.BlockSpec((1, LANES), lambda i, j: (0, 0))],
        out_specs=[pl.BlockSpec((tm, tn), lambda i, j: (i, jnp.minimum(j, n_f32 - 1))),
                   pl.BlockSpec((tm, tn), lambda i, j: (i, 0)),
                   pl.BlockSpec((tm, LANES), lambda i, j: (i, 0))],
        out_shape=[jax.ShapeDtypeStruct((t, n_f32 * tn), F32),
                   jax.ShapeDtypeStruct((t, bc_w), BF16),
                   jax.ShapeDtypeStruct((t, LANES), F32)],
        scratch_shapes=[pltpu.VMEM((tm + CONV_CARRY_ROWS, tn), F32),
                        pltpu.VMEM((n_x + 1, CONV_CARRY_ROWS, tn), F32)],
        compiler_params=_cparams(2),
        name="proj_zxg",
    )(h, w_main, w_dt, conv_w, conv_b, dt_bias)


def _qkv_kernel(h_ref, w_ref, hnw_ref, cos_ref, sin_ref, bd_ref,
                q_ref, k_ref, v_ref, res_scr, tmp_scr, *, dil, tm):
    j = pl.program_id(1)
    n = tm // dil
    n_chunks = ATT_GROUP_COLS // LANES
    res = jnp.dot(h_ref[...], w_ref[...], preferred_element_type=F32)
    for c in range(n_chunks):
        res_scr[c] = res[:, c * LANES:(c + 1) * LANES]

    def store_classes(out_ref):
        for c in range(n_chunks):
            cols = slice(c * LANES, (c + 1) * LANES)
            if dil == 1:
                out_ref[0, :, cols] = res_scr[c].astype(BF16)
            elif dil <= DEINTERLEAVE_STRIDE:
                for r in range(dil):
                    out_ref[r, :, cols] = res_scr[c, pl.ds(r, n, stride=dil), :].astype(BF16)
            else:
                s1 = DEINTERLEAVE_STRIDE
                s2 = dil // s1
                n1 = tm // s1
                for r1 in range(s1):
                    tmp_scr[pl.ds(r1 * n1, n1), :] = res_scr[c, pl.ds(r1, n1, stride=s1), :]
                for r1 in range(s1):
                    for r2 in range(s2):
                        out_ref[r1 + s1 * r2, :, cols] = (
                            tmp_scr[pl.ds(r1 * n1 + r2, n, stride=s2), :].astype(BF16))

    def norm_rope(out_ref):
        lane = lax.broadcasted_iota(jnp.int32, (tm, LANES), 1)
        first_half = (lane % HEAD_DIM) < (HEAD_DIM // 2)
        for c in range(n_chunks):
            xr = res_scr[c]
            x2 = xr * xr
            hi = x2.astype(BF16)
            lo = (x2 - hi.astype(F32)).astype(BF16)
            ms = (jnp.dot(hi, bd_ref[...], preferred_element_type=F32)
                  + jnp.dot(lo, bd_ref[...], preferred_element_type=F32))
            y = xr * lax.rsqrt(ms + NORM_EPS) * hnw_ref[...]
            partner = jnp.where(first_half,
                                pltpu.roll(y, LANES - HEAD_DIM // 2, 1),
                                pltpu.roll(y, HEAD_DIM // 2, 1))
            res_scr[c] = y * cos_ref[...] + partner * sin_ref[...]
        store_classes(out_ref)

    @pl.when(j == 0)
    def _():
        norm_rope(q_ref)

    @pl.when(j == 1)
    def _():
        norm_rope(k_ref)

    @pl.when(j == 2)
    def _():
        store_classes(v_ref)


def _qkv(h, w, hnw, cos, sin, bd, *, bsz, seq, dil, tm=1024):
    t, d = h.shape
    tiles_per_seq = seq // tm
    n = tm // dil
    gc = ATT_GROUP_COLS
    assert dil <= DEINTERLEAVE_STRIDE or dil % DEINTERLEAVE_STRIDE == 0
    out_sds = jax.ShapeDtypeStruct((bsz, dil, seq // dil, gc), BF16)
    out_spec = pl.BlockSpec((None, dil, n, gc),
                            lambda i, j: (i // tiles_per_seq, 0, i % tiles_per_seq, 0))
    return pl.pallas_call(
        functools.partial(_qkv_kernel, dil=dil, tm=tm),
        grid=(t // tm, 3),
        in_specs=[pl.BlockSpec((tm, d), lambda i, j: (i, 0)),
                  pl.BlockSpec((d, gc), lambda i, j: (0, j)),
                  pl.BlockSpec((None, 1, LANES), lambda i, j: (jnp.minimum(j, 1), 0, 0)),
                  pl.BlockSpec((tm, LANES), lambda i, j: (i % tiles_per_seq, 0)),
                  pl.BlockSpec((tm, LANES), lambda i, j: (i % tiles_per_seq, 0)),
                  pl.BlockSpec((LANES, LANES), lambda i, j: (0, 0))],
        out_specs=[out_spec, out_spec, out_spec],
        out_shape=[out_sds, out_sds, out_sds],
        scratch_shapes=[pltpu.VMEM((gc // LANES, tm, LANES), F32), pltpu.VMEM((tm, LANES), F32)],
        compiler_params=_cparams(2),
        name=f"qkv_d{dil}",
    )(h, w, hnw, cos, sin, bd)


def _attn_kernel(q0, k0, v0, q1, k1, v1, q2, k2, v2, o_ref, acc_o, acc_m, acc_l, *, seq):
    band = ATT_BAND
    nbatch = ATT_BLOCK_BATCH
    lane = lax.broadcasted_iota(jnp.int32, (band, LANES), 1)
    head0 = lane < HEAD_DIM
    iq = lax.broadcasted_iota(jnp.int32, (band, band), 0)
    ik = lax.broadcasted_iota(jnp.int32, (band, band), 1)
    cur_mask = ik <= iq
    prev_mask = ik >= iq
    nt_dims = (((2,), (2,)), ((0,), (0,)))
    nn_dims = (((2,), (1,)), ((0,), (0,)))

    ones = jnp.ones((nbatch, band, LANES), BF16)
    cur_mask2 = jnp.concatenate([cur_mask, cur_mask], axis=0)
    prev_mask2 = jnp.concatenate([prev_mask, prev_mask], axis=0)

    def blocks(qb, kc, vc, kp=None, vp=None, first_has_no_prev=False):
        zero = jnp.zeros_like(qb)
        qs = jnp.concatenate([jnp.where(head0, qb, zero), jnp.where(head0, zero, qb)], axis=1)
        sc = lax.dot_general(qs, kc, nt_dims, preferred_element_type=F32)
        sc = jnp.where(cur_mask2, sc, -jnp.inf)
        m = jnp.max(sc, axis=-1, keepdims=True)
        if kp is not None:
            sp = lax.dot_general(qs, kp, nt_dims, preferred_element_type=F32)
            sp = jnp.where(prev_mask2, sp, -jnp.inf)
            if first_has_no_prev:
                bidx = lax.broadcasted_iota(jnp.int32, sp.shape, 0)
                sp = jnp.where(bidx > 0, sp, -jnp.inf)
            m = jnp.maximum(m, jnp.max(sp, axis=-1, keepdims=True))
        acc = lax.dot_general(jnp.exp(sc - m).astype(BF16), jnp.concatenate([vc, ones], axis=-1),
                              nn_dims, preferred_element_type=F32)
        if kp is not None:
            acc = acc + lax.dot_general(jnp.exp(sp - m).astype(BF16),
                                        jnp.concatenate([vp, ones], axis=-1),
                                        nn_dims, preferred_element_type=F32)
        top, bot = acc[:, :band], acc[:, band:]
        return (jnp.where(head0, top[..., :LANES], bot[..., :LANES]),
                jnp.where(head0, m[:, :band], m[:, band:]),
                jnp.where(head0, top[..., LANES:], bot[..., LANES:]))

    def store(g, rows, res, b=None):
        for acc, val in zip((acc_o, acc_m, acc_l), res):
            acc[g, rows, :] = val.reshape(nbatch * band, LANES) if b is None else val[b]

    span = nbatch * band
    for n0 in range(0, seq // band, nbatch):
        def view(ref, start):
            return ref[0, pl.ds(start, span), :].reshape(nbatch, band, LANES)
        if n0 == 0:
            def shifted(ref):
                return jnp.concatenate([ref[0, pl.ds(0, band), :], ref[0, pl.ds(0, span - band), :]],
                                       axis=0).reshape(nbatch, band, LANES)
            kp, vp = shifted(k0), shifted(v0)
        else:
            kp, vp = view(k0, (n0 - 1) * band), view(v0, (n0 - 1) * band)
        res = blocks(view(q0, n0 * band), view(k0, n0 * band), view(v0, n0 * band), kp, vp,
                     first_has_no_prev=(n0 == 0))
        store(0, pl.ds(n0 * band, span), res)

    dil = ATT_DILATIONS[1]
    assert dil == nbatch
    for n in range(seq // dil // band):
        cur = pl.ds(n * band, band)
        if n == 0:
            res = blocks(q1[:, cur, :], k1[:, cur, :], v1[:, cur, :])
        else:
            prev = pl.ds((n - 1) * band, band)
            res = blocks(q1[:, cur, :], k1[:, cur, :], v1[:, cur, :], k1[:, prev, :], v1[:, prev, :])
        for r in range(dil):
            store(1, pl.ds(r + dil * band * n, band, stride=dil), res, b=r)

    dil = ATT_DILATIONS[2]
    assert seq // dil == band
    for r0 in range(0, dil, nbatch):
        cls = pl.ds(r0, nbatch)
        res = blocks(q2[cls], k2[cls], v2[cls])
        for b in range(nbatch):
            store(2, pl.ds(r0 + b, band, stride=dil), res, b=b)

    def merge_rows(i, carry):
        rows = pl.ds(pl.multiple_of(i * band, band), band)
        m0, m1, m2 = acc_m[0, rows, :], acc_m[1, rows, :], acc_m[2, rows, :]
        m = jnp.maximum(jnp.maximum(m0, m1), m2)
        w0, w1, w2 = jnp.exp(m0 - m), jnp.exp(m1 - m), jnp.exp(m2 - m)
        num = w0 * acc_o[0, rows, :] + w1 * acc_o[1, rows, :] + w2 * acc_o[2, rows, :]
        den = w0 * acc_l[0, rows, :] + w1 * acc_l[1, rows, :] + w2 * acc_l[2, rows, :]
        o_ref[rows, :] = (num / den).astype(o_ref.dtype)
        return carry

    lax.fori_loop(0, seq // band, merge_rows, 0)


def _attention(qkv, *, bsz, seq):
    n_pairs = ATT_GROUP_COLS // LANES
    in_specs = []
    for dil in ATT_DILATIONS:
        spec = pl.BlockSpec((None, dil, seq // dil, LANES), lambda b, p: (b, 0, 0, p))
        in_specs += [spec, spec, spec]
    return pl.pallas_call(
        functools.partial(_attn_kernel, seq=seq),
        grid=(bsz, n_pairs),
        in_specs=in_specs,
        out_specs=pl.BlockSpec((None, seq, LANES), lambda b, p: (b, 0, p)),
        out_shape=jax.ShapeDtypeStruct((bsz, seq, ATT_GROUP_COLS), BF16),
        scratch_shapes=[pltpu.VMEM((len(ATT_DILATIONS), seq, LANES), F32)] * 3,
        compiler_params=_cparams(2),
        name="dilated_attn",
    )(*qkv)


def _ssd_kernel(sz_ref, x_ref, bc_ref, dt_ref, alog_ref, dskip_ref, nw_ref, o_ref,
                st_scr, y_scr, cum_scr, cumt_scr, dtt_scr, swt_scr):
    L = SSD_CHUNK
    n_state = D_STATE
    d_inner = x_ref.shape[-1]
    n_heads = d_inner // SSM_HEAD_DIM
    heads_per_group = n_heads // N_SSM_GROUPS
    group_w = d_inner // N_SSM_GROUPS

    @pl.when(pl.program_id(1) == 0)
    def _():
        st_scr[...] = jnp.zeros_like(st_scr)

    dt = dt_ref[...]
    a = -jnp.exp(alog_ref[...])
    ti = lax.broadcasted_iota(jnp.int32, (L, L), 0)
    tj = lax.broadcasted_iota(jnp.int32, (L, L), 1)
    causal = ti >= tj
    cum = jnp.dot(causal.astype(F32), dt * a, preferred_element_type=F32,
                  precision=lax.Precision.HIGHEST)
    cum_t = cum.T
    dt_t = dt.T
    cum_scr[...] = cum
    cumt_scr[...] = cum_t
    dtt_scr[...] = dt_t
    swt_scr[...] = dt_t * jnp.exp(cum_t[:, L - 1:L] - cum_t)

    lane = lax.broadcasted_iota(jnp.int32, (L, LANES), 1)
    head0 = lane < SSM_HEAD_DIM

    for g in range(N_SSM_GROUPS):
        b_g = bc_ref[:, g * n_state:(g + 1) * n_state]
        c_g = bc_ref[:, (N_SSM_GROUPS + g) * n_state:(N_SSM_GROUPS + g + 1) * n_state]
        bt_g = b_g.astype(F32).T
        c_gf = c_g.astype(F32)
        cb = jnp.dot(c_g, bt_g.astype(BF16), preferred_element_type=F32)
        for pair in range(heads_per_group // 2):
            col = g * group_w + pair * LANES
            cols = slice(col, col + LANES)
            x_pair = x_ref[:, cols]
            x_pair_b = x_pair.astype(BF16)
            st_pair = st_scr[:, cols]
            rhs = jnp.concatenate([x_pair_b, st_pair.astype(BF16)], axis=0)
            ys, ds, es = [], [], []
            for h in (g * heads_per_group + 2 * pair, g * heads_per_group + 2 * pair + 1):
                colb = jnp.broadcast_to(cum_scr[:, h:h + 1], (L, L))
                rowb = cumt_scr[h:h + 1, :]
                dec = jnp.exp(jnp.where(causal, colb - rowb, -jnp.inf))
                m_h = (cb * dec * dtt_scr[h:h + 1, :]).astype(BF16)
                c_h = (c_gf * jnp.exp(colb)).astype(BF16)
                ys.append(jnp.dot(jnp.concatenate([m_h, c_h], axis=1), rhs,
                                  preferred_element_type=F32))
                bt_h = (bt_g * swt_scr[h:h + 1, :]).astype(BF16)
                ds.append(jnp.dot(bt_h, x_pair_b, preferred_element_type=F32))
                es.append(jnp.exp(cumt_scr[h:h + 1, L - 1:L]))
            st_scr[:, cols] = (st_pair * jnp.where(head0, es[0], es[1])
                               + jnp.where(head0, ds[0], ds[1]))
            y_scr[:, cols] = jnp.where(head0, ys[0], ys[1]) + dskip_ref[:, cols] * x_pair

    for g in range(N_SSM_GROUPS):
        cols = slice(g * group_w, (g + 1) * group_w)
        y = y_scr[:, cols] * sz_ref[:, cols]
        o_ref[:, cols] = _rms_norm(y, nw_ref[:, cols]).astype(o_ref.dtype)


def _ssd(zxg, bc, dt, alog, dskip, nw, *, bsz, seq, d_inner):
    L = SSD_CHUNK
    bc_w = bc.shape[-1]
    zxg3 = zxg.reshape(bsz, seq, zxg.shape[-1])

    def small(shape):
        return pl.BlockSpec(shape, lambda b, c: (0,) * len(shape))

    out = pl.pallas_call(
        _ssd_kernel,
        grid=(bsz, seq // L),
        in_specs=[pl.BlockSpec((None, L, d_inner), lambda b, c: (b, c, 0)),
                  pl.BlockSpec((None, L, d_inner), lambda b, c: (b, c, 1)),
                  pl.BlockSpec((None, L, bc_w), lambda b, c: (b, c, 0)),
                  pl.BlockSpec((None, L, LANES), lambda b, c: (b, c, 0)),
                  small((1, LANES)), small((1, d_inner)), small((1, d_inner))],
        out_specs=pl.BlockSpec((None, L, d_inner), lambda b, c: (b, c, 0)),
        out_shape=jax.ShapeDtypeStruct((bsz, seq, d_inner), BF16),
        scratch_shapes=[pltpu.VMEM((D_STATE, d_inner), F32),
                        pltpu.VMEM((L, d_inner), F32),
                        pltpu.VMEM((L, LANES), F32),
                        pltpu.VMEM((LANES, L), F32),
                        pltpu.VMEM((LANES, L), F32),
                        pltpu.VMEM((LANES, L), F32)],
        compiler_params=_cparams(2),
        name="ssd",
    )(zxg3, zxg3, bc.reshape(bsz, seq, bc_w), dt.reshape(bsz, seq, LANES), alog, dskip, nw)
    return out.reshape(bsz * seq, d_inner)


def _merge_kernel(x_ref, att_ref, yn_ref, gp_ref, bg_ref, wa_ref, ws_ref, wo_ref, o_ref):
    d = x_ref.shape[-1]
    ya = jnp.dot(att_ref[...], wa_ref[...], preferred_element_type=F32)
    ys = jnp.dot(yn_ref[...], ws_ref[...], preferred_element_type=F32)
    gates = _sigmoid(gp_ref[...] + bg_ref[...])
    mixed = (gates[:, :d] * ya + gates[:, d:] * ys).astype(BF16)
    o_ref[...] = x_ref[...] + jnp.dot(mixed, wo_ref[...], preferred_element_type=F32)


def _merge(x, att, yn, zxg, bg, wa, ws, wo, tm=512):
    t, d = x.shape
    d_inner = yn.shape[1]
    assert (2 * d_inner) % (2 * d) == 0
    gate_blk = (2 * d_inner) // (2 * d)
    return pl.pallas_call(
        _merge_kernel,
        grid=(t // tm,),
        in_specs=[pl.BlockSpec((tm, d), lambda i: (i, 0)),
                  pl.BlockSpec((tm, att.shape[1]), lambda i: (i, 0)),
                  pl.BlockSpec((tm, yn.shape[1]), lambda i: (i, 0)),
                  pl.BlockSpec((tm, 2 * d), lambda i: (i, gate_blk)),
                  _resident((1, 2 * d)), _resident(wa.shape), _resident(ws.shape), _resident(wo.shape)],
        out_specs=pl.BlockSpec((tm, d), lambda i: (i, 0)),
        out_shape=jax.ShapeDtypeStruct((t, d), F32),
        compiler_params=_cparams(1),
        name="merge",
    )(x, att, yn, zxg, bg, wa, ws, wo)


def _rope_tables(seq):
    pos = jnp.arange(seq, dtype=F32)
    inv_freq = 1.0 / (ROPE_THETA ** (jnp.arange(0, HEAD_DIM, 2, dtype=F32) / HEAD_DIM))
    ang = pos[:, None] * inv_freq[None, :]
    cos, sin = jnp.cos(ang), jnp.sin(ang)
    reps = LANES // HEAD_DIM
    return (jnp.tile(jnp.concatenate([cos, cos], axis=1), (1, reps)),
            jnp.tile(jnp.concatenate([-sin, sin], axis=1), (1, reps)))


def kernel(x, ffn1_norm_w, ffn1_w_gate, ffn1_w_up, ffn1_w_down, mix_norm_w, w_in, b_gates,
           q_norm_w, k_norm_w, conv_w, conv_b, dt_bias, a_log, d_skip, ssm_norm_w,
           w_att_proj, w_ssm_proj, w_out, ffn2_norm_w, ffn2_w_gate, ffn2_w_up, ffn2_w_down):
    bsz, seq, d_model = x.shape
    depth = w_in.shape[0]
    d_inner = ssm_norm_w.shape[1]
    n_ssm_heads = dt_bias.shape[1]
    qkv_cols = len(ATT_DILATIONS) * ATT_GROUP_COLS
    bc_w = 2 * N_SSM_GROUPS * D_STATE
    assert seq % (ATT_DILATIONS[-1] * ATT_BAND) == 0 and seq % 1024 == 0
    assert w_in.shape[2] == 3 * qkv_cols + d_inner + (d_inner + bc_w) + n_ssm_heads + 2 * d_model
    assert n_ssm_heads <= LANES

    o_z = 3 * qkv_cols
    o_x = o_z + d_inner
    o_bc = o_x + d_inner
    o_dt = o_bc + bc_w
    o_g = o_dt + n_ssm_heads
    w_in_b = w_in.astype(BF16)
    gc = ATT_GROUP_COLS
    w_groups = [jnp.concatenate([w_in_b[:, :, s * qkv_cols + g * gc: s * qkv_cols + (g + 1) * gc]
                                 for s in range(3)], axis=2) for g in range(len(ATT_DILATIONS))]
    w_main = jnp.concatenate([w_in_b[:, :, o_z:o_bc], w_in_b[:, :, o_g:], w_in_b[:, :, o_bc:o_dt]], axis=2)
    pad_h = LANES - n_ssm_heads
    w_dt = jnp.pad(w_in_b[:, :, o_dt:o_g], ((0, 0), (0, 0), (0, pad_h)))
    zeros_zg = jnp.zeros((depth, CONV_WIDTH, d_inner), F32)
    conv_w_m = jnp.concatenate([zeros_zg, conv_w[:, :, :d_inner], jnp.zeros((depth, CONV_WIDTH, 2 * d_model), F32),
                                conv_w[:, :, d_inner:]], axis=2)
    conv_b_m = jnp.concatenate([zeros_zg[:, :1], conv_b[:, None, :d_inner],
                                jnp.zeros((depth, 1, 2 * d_model), F32), conv_b[:, None, d_inner:]], axis=2)
    f1g, f1u, f1d = ffn1_w_gate.astype(BF16), ffn1_w_up.astype(BF16), ffn1_w_down.astype(BF16)
    f2g, f2u, f2d = ffn2_w_gate.astype(BF16), ffn2_w_up.astype(BF16), ffn2_w_down.astype(BF16)
    wa_b, ws_b, wo_b = w_att_proj.astype(BF16), w_ssm_proj.astype(BF16), w_out.astype(BF16)

    reps = LANES // HEAD_DIM
    head_nw = jnp.stack([jnp.tile(q_norm_w * (HEAD_DIM ** -0.5), (1, reps)),
                         jnp.tile(k_norm_w, (1, reps))], axis=1)[:, :, None, :]
    cos_t, sin_t = _rope_tables(seq)
    hid = jnp.arange(LANES) // HEAD_DIM
    block_diag = jnp.where(hid[:, None] == hid[None, :], 1.0 / HEAD_DIM, 0.0).astype(BF16)
    dtb_p = jnp.pad(dt_bias, ((0, 0), (0, pad_h)))[:, None, :]
    alog_p = jnp.pad(a_log, ((0, 0), (0, pad_h)))[:, None, :]
    dskip_e = jnp.repeat(d_skip, SSM_HEAD_DIM, axis=1)[:, None, :]

    xt = x.reshape(bsz * seq, d_model)
    for i in range(depth):
        xt, h = _ffn(xt, ffn1_norm_w[i][None], f1g[i], f1u[i], f1d[i], mix_norm_w[i][None])
        qkv = []
        for g, dil in enumerate(ATT_DILATIONS):
            qkv += _qkv(h, w_groups[g][i], head_nw[i], cos_t, sin_t, block_diag,
                        bsz=bsz, seq=seq, dil=dil)
        zxg, bc, dt = _proj(h, w_main[i], w_dt[i], conv_w_m[i], conv_b_m[i], dtb_p[i],
                            seq=seq, d_inner=d_inner, d_model=d_model)
        att = _attention(qkv, bsz=bsz, seq=seq).reshape(bsz * seq, ATT_GROUP_COLS)
        yn = _ssd(zxg, bc, dt, alog_p[i], dskip_e[i], ssm_norm_w[i][None],
                  bsz=bsz, seq=seq, d_inner=d_inner)
        xt = _merge(xt, att, yn, zxg, b_gates[i][None], wa_b[i], ws_b[i], wo_b[i])
        xt = _ffn(xt, ffn2_norm_w[i][None], f2g[i], f2u[i], f2d[i])
    return xt.reshape(bsz, seq, d_model)
```

```python
import functools

import jax
import jax.numpy as jnp
from jax import lax
from jax.experimental import pallas as pl
from jax.experimental.pallas import tpu as pltpu

F32 = jnp.float32
BF16 = jnp.bfloat16

NORM_EPS = 1e-6
ROPE_THETA = 10000.0
HEAD_DIM = 64
ATT_DILATIONS = (1, 4, 16)
ATT_BAND = 128
DEINTERLEAVE_STRIDE = 4
ATT_BLOCK_BATCH = 8
HEADS_PER_GROUP = 8
ATT_GROUP_COLS = HEADS_PER_GROUP * HEAD_DIM
SSM_HEAD_DIM = 64
N_SSM_GROUPS = 4
D_STATE = 128
CONV_WIDTH = 4
SSD_CHUNK = 128
LANES = 128
CONV_CARRY_ROWS = 8
PROJ_CONV_ROWS = 256
PROJ_TN = 1024
VMEM_LIMIT = 56 * 1024 * 1024


def _cparams(n_axes):
    return pltpu.CompilerParams(dimension_semantics=("arbitrary",) * n_axes,
                                vmem_limit_bytes=VMEM_LIMIT)


def _rms_norm(x, w):
    return x * lax.rsqrt(jnp.mean(x * x, axis=-1, keepdims=True) + NORM_EPS) * w


def _sigmoid(x):
    return 0.5 * jnp.tanh(0.5 * x) + 0.5


def _silu(x):
    h = 0.5 * x
    return h * jnp.tanh(h) + h


def _softplus(x):
    return jnp.maximum(x, 0.0) + jnp.log1p(jnp.exp(-jnp.abs(x)))


def _resident(shape):
    return pl.BlockSpec(shape, lambda *_: (0,) * len(shape), pipeline_mode=pl.Buffered(1))


def _ffn_kernel(x_ref, nw_ref, wg_ref, wu_ref, wd_ref, *rest):
    x = x_ref[...]
    h = _rms_norm(x, nw_ref[...]).astype(BF16)
    g = jnp.dot(h, wg_ref[...], preferred_element_type=F32)
    u = jnp.dot(h, wu_ref[...], preferred_element_type=F32)
    a = (_silu(g) * u).astype(BF16)
    y = jnp.dot(a, wd_ref[...], preferred_element_type=F32)
    out = x + 0.5 * y
    if len(rest) == 1:
        rest[0][...] = out
    else:
        nw_next_ref, o_ref, h_ref = rest
        o_ref[...] = out
        h_ref[...] = _rms_norm(out, nw_next_ref[...]).astype(BF16)


def _ffn(x, nw, wg, wu, wd, nw_next=None, tm=512):
    t, d = x.shape
    f = wg.shape[1]
    row = pl.BlockSpec((tm, d), lambda i: (i, 0))
    in_specs = [row, _resident((1, d)), _resident((d, f)), _resident((d, f)), _resident((f, d))]
    args = [x, nw, wg, wu, wd]
    out_specs, out_shape = row, jax.ShapeDtypeStruct((t, d), F32)
    if nw_next is not None:
        in_specs.append(_resident((1, d)))
        args.append(nw_next)
        out_specs, out_shape = [row, row], [out_shape, jax.ShapeDtypeStruct((t, d), BF16)]
    return pl.pallas_call(
        _ffn_kernel,
        grid=(t // tm,),
        in_specs=in_specs,
        out_specs=out_specs,
        out_shape=out_shape,
        compiler_params=_cparams(1),
        name="ffn",
    )(*args)


def _proj_kernel(h_ref, w_ref, wdt_ref, cw_ref, cb_ref, dtb_ref, zxg_ref, bc_ref, dt_ref,
                 conv_scr, conv_scr2, hist_scr, *, tm, tiles_per_seq, n_z, n_x, n_g):
    i = pl.program_id(0)
    j = pl.program_id(1)
    hist = CONV_CARRY_ROWS
    sub = PROJ_CONV_ROWS

    def matmul(rows=None):
        lhs = h_ref[...] if rows is None else h_ref[rows, :]
        return jnp.dot(lhs, w_ref[...], preferred_element_type=F32)

    def conv_silu(slot, out_ref):
        bufs = (conv_scr, conv_scr2)

        @pl.when(i % tiles_per_seq == 0)
        def _():
            conv_scr[pl.ds(0, hist), :] = jnp.zeros((hist, conv_scr.shape[1]), F32)

        @pl.when(i % tiles_per_seq != 0)
        def _():
            conv_scr[pl.ds(0, hist), :] = hist_scr[slot]

        n_sub = tm // sub
        for s in range(n_sub):
            buf = bufs[s % 2]
            buf[pl.ds(hist, sub), :] = matmul(pl.ds(s * sub, sub))
            rows = buf[...]
            acc = cb_ref[...] + cw_ref[CONV_WIDTH - 1:CONV_WIDTH, :] * rows[hist:]
            for k in range(1, CONV_WIDTH):
                acc = acc + (cw_ref[CONV_WIDTH - 1 - k:CONV_WIDTH - k, :]
                             * pltpu.roll(rows, k, 0)[hist:])
            out_ref[pl.ds(s * sub, sub), :] = _silu(acc).astype(out_ref.dtype)
            tail = buf[pl.ds(sub, hist), :]
            if s + 1 < n_sub:
                bufs[(s + 1) % 2][pl.ds(0, hist), :] = tail
            else:
                hist_scr[slot] = tail

    for jj in range(n_z + n_x + n_g + 1):
        @pl.when(j == jj)
        def _(jj=jj):
            if jj < n_z:
                zxg_ref[...] = _silu(matmul())
            elif jj < n_z + n_x:
                conv_silu(jj - n_z, zxg_ref)
            elif jj < n_z + n_x + n_g:
                zxg_ref[...] = matmul()
            else:
                conv_silu(n_x, bc_ref)
                dt_raw = jnp.dot(h_ref[...], wdt_ref[...], preferred_element_type=F32)
                dt_ref[...] = _softplus(dt_raw + dtb_ref[...])


def _proj(h, w_main, w_dt, conv_w, conv_b, dt_bias, *, seq, d_inner, d_model, tm=1024, tn=PROJ_TN):
    t, d = h.shape
    bc_w = 2 * N_SSM_GROUPS * D_STATE
    assert bc_w == tn and d_inner % tn == 0 and (2 * d_model) % tn == 0
    n_z, n_x, n_g = d_inner // tn, d_inner // tn, 2 * d_model // tn
    n_f32 = n_z + n_x + n_g
    return pl.pallas_call(
        functools.partial(_proj_kernel, tm=tm, tiles_per_seq=seq // tm, n_z=n_z, n_x=n_x, n_g=n_g),
        grid=(t // tm, n_f32 + 1),
        in_specs=[pl.BlockSpec((tm, d), lambda i, j: (i, 0)),
                  pl.BlockSpec((d, tn), lambda i, j: (0, j)),
                  pl.BlockSpec((d, LANES), lambda i, j: (0, 0)),
                  pl.BlockSpec((CONV_WIDTH, tn), lambda i, j: (0, j)),
                  pl.BlockSpec((1, tn), lambda i, j: (0, j)),
                  pl.BlockSpec((1, LANES), lambda i, j: (0, 0))],
        out_specs=[pl.BlockSpec((tm, tn), lambda i, j: (i, jnp.minimum(j, n_f32 - 1))),
                   pl.BlockSpec((tm, tn), lambda i, j: (i, 0)),
                   pl.BlockSpec((tm, LANES), lambda i, j: (i, 0))],
        out_shape=[jax.ShapeDtypeStruct((t, n_f32 * tn), F32),
                   jax.ShapeDtypeStruct((t, bc_w), BF16),
                   jax.ShapeDtypeStruct((t, LANES), F32)],
        scratch_shapes=[pltpu.VMEM((PROJ_CONV_ROWS + CONV_CARRY_ROWS, tn), F32),
                        pltpu.VMEM((PROJ_CONV_ROWS + CONV_CARRY_ROWS, tn), F32),
                        pltpu.VMEM((n_x + 1, CONV_CARRY_ROWS, tn), F32)],
        compiler_params=_cparams(2),
        name="proj_zxg",
    )(h, w_main, w_dt, conv_w, conv_b, dt_bias)


def _qkv_kernel(h_ref, w_ref, hnw_ref, cos_ref, sin_ref, bd_ref,
                q_ref, k_ref, v_ref, res_scr, tmp_scr, *, dil, tm):
    j = pl.program_id(1)
    n = tm // dil
    n_chunks = ATT_GROUP_COLS // LANES
    res = jnp.dot(h_ref[...], w_ref[...], preferred_element_type=F32)
    for c in range(n_chunks):
        res_scr[c] = res[:, c * LANES:(c + 1) * LANES]

    def store_classes(out_ref):
        for c in range(n_chunks):
            cols = slice(c * LANES, (c + 1) * LANES)
            if dil == 1:
                out_ref[0, :, cols] = res_scr[c].astype(BF16)
            elif dil <= DEINTERLEAVE_STRIDE:
                for r in range(dil):
                    out_ref[r, :, cols] = res_scr[c, pl.ds(r, n, stride=dil), :].astype(BF16)
            else:
                s1 = DEINTERLEAVE_STRIDE
                s2 = dil // s1
                n1 = tm // s1
                for r1 in range(s1):
                    tmp_scr[pl.ds(r1 * n1, n1), :] = res_scr[c, pl.ds(r1, n1, stride=s1), :]
                for r1 in range(s1):
                    for r2 in range(s2):
                        out_ref[r1 + s1 * r2, :, cols] = (
                            tmp_scr[pl.ds(r1 * n1 + r2, n, stride=s2), :].astype(BF16))

    def norm_rope(out_ref):
        lane = lax.broadcasted_iota(jnp.int32, (tm, LANES), 1)
        first_half = (lane % HEAD_DIM) < (HEAD_DIM // 2)
        for c in range(n_chunks):
            xr = res_scr[c]
            x2 = xr * xr
            hi = x2.astype(BF16)
            lo = (x2 - hi.astype(F32)).astype(BF16)
            ms = (jnp.dot(hi, bd_ref[...], preferred_element_type=F32)
                  + jnp.dot(lo, bd_ref[...], preferred_element_type=F32))
            y = xr * lax.rsqrt(ms + NORM_EPS) * hnw_ref[...]
            partner = jnp.where(first_half,
                                pltpu.roll(y, LANES - HEAD_DIM // 2, 1),
                                pltpu.roll(y, HEAD_DIM // 2, 1))
            res_scr[c] = y * cos_ref[...] + partner * sin_ref[...]
        store_classes(out_ref)

    @pl.when(j == 0)
    def _():
        norm_rope(q_ref)

    @pl.when(j == 1)
    def _():
        norm_rope(k_ref)

    @pl.when(j == 2)
    def _():
        store_classes(v_ref)


def _qkv(h, w, hnw, cos, sin, bd, *, bsz, seq, dil, tm=1024):
    t, d = h.shape
    tiles_per_seq = seq // tm
    n = tm // dil
    gc = ATT_GROUP_COLS
    assert dil <= DEINTERLEAVE_STRIDE or dil % DEINTERLEAVE_STRIDE == 0
    out_sds = jax.ShapeDtypeStruct((bsz, dil, seq // dil, gc), BF16)
    out_spec = pl.BlockSpec((None, dil, n, gc),
                            lambda i, j: (i // tiles_per_seq, 0, i % tiles_per_seq, 0))
    return pl.pallas_call(
        functools.partial(_qkv_kernel, dil=dil, tm=tm),
        grid=(t // tm, 3),
        in_specs=[pl.BlockSpec((tm, d), lambda i, j: (i, 0)),
                  pl.BlockSpec((d, gc), lambda i, j: (0, j)),
                  pl.BlockSpec((None, 1, LANES), lambda i, j: (jnp.minimum(j, 1), 0, 0)),
                  pl.BlockSpec((tm, LANES), lambda i, j: (i % tiles_per_seq, 0)),
                  pl.BlockSpec((tm, LANES), lambda i, j: (i % tiles_per_seq, 0)),
                  pl.BlockSpec((LANES, LANES), lambda i, j: (0, 0))],
        out_specs=[out_spec, out_spec, out_spec],
        out_shape=[out_sds, out_sds, out_sds],
        scratch_shapes=[pltpu.VMEM((gc // LANES, tm, LANES), F32), pltpu.VMEM((tm, LANES), F32)],
        compiler_params=_cparams(2),
        name=f"qkv_d{dil}",
    )(h, w, hnw, cos, sin, bd)


def _attn_kernel(q0, k0, v0, q1, k1, v1, q2, k2, v2, o_ref, acc_o, acc_m, acc_l, *, seq):
    band = ATT_BAND
    nbatch = ATT_BLOCK_BATCH
    lane = lax.broadcasted_iota(jnp.int32, (band, LANES), 1)
    head0 = lane < HEAD_DIM
    iq = lax.broadcasted_iota(jnp.int32, (band, band), 0)
    ik = lax.broadcasted_iota(jnp.int32, (band, band), 1)
    cur_mask = ik <= iq
    prev_mask = ik >= iq
    nt_dims = (((2,), (2,)), ((0,), (0,)))
    nn_dims = (((2,), (1,)), ((0,), (0,)))

    cur_mask2 = jnp.concatenate([cur_mask, cur_mask], axis=0)
    prev_mask2 = jnp.concatenate([prev_mask, prev_mask], axis=0)

    def blocks(qb, kc, vc, kp=None, vp=None, first_has_no_prev=False):
        zero = jnp.zeros_like(qb)
        ones = jnp.ones(vc.shape, BF16)
        qs = jnp.concatenate([jnp.where(head0, qb, zero), jnp.where(head0, zero, qb)], axis=1)
        sc = lax.dot_general(qs, kc, nt_dims, preferred_element_type=F32)
        sc = jnp.where(cur_mask2, sc, -jnp.inf)
        m = jnp.max(sc, axis=-1, keepdims=True)
        if kp is not None:
            sp = lax.dot_general(qs, kp, nt_dims, preferred_element_type=F32)
            sp = jnp.where(prev_mask2, sp, -jnp.inf)
            if first_has_no_prev:
                bidx = lax.broadcasted_iota(jnp.int32, sp.shape, 0)
                sp = jnp.where(bidx > 0, sp, -jnp.inf)
            m = jnp.maximum(m, jnp.max(sp, axis=-1, keepdims=True))
        acc = lax.dot_general(jnp.exp(sc - m).astype(BF16), jnp.concatenate([vc, ones], axis=-1),
                              nn_dims, preferred_element_type=F32)
        if kp is not None:
            acc = acc + lax.dot_general(jnp.exp(sp - m).astype(BF16),
                                        jnp.concatenate([vp, ones], axis=-1),
                                        nn_dims, preferred_element_type=F32)
        top, bot = acc[:, :band], acc[:, band:]
        return (jnp.where(head0, top[..., :LANES], bot[..., :LANES]),
                jnp.where(head0, m[:, :band], m[:, band:]),
                jnp.where(head0, top[..., LANES:], bot[..., LANES:]))

    def store(g, rows, res, b=None):
        for acc, val in zip((acc_o, acc_m, acc_l), res):
            acc[g, rows, :] = val.reshape(-1, LANES) if b is None else val[b]

    span = nbatch * band
    for n0 in range(0, seq // band, nbatch):
        def view(ref, start):
            return ref[0, pl.ds(start, span), :].reshape(nbatch, band, LANES)
        if n0 == 0:
            def shifted(ref):
                return jnp.concatenate([ref[0, pl.ds(0, band), :], ref[0, pl.ds(0, span - band), :]],
                                       axis=0).reshape(nbatch, band, LANES)
            kp, vp = shifted(k0), shifted(v0)
        else:
            kp, vp = view(k0, (n0 - 1) * band), view(v0, (n0 - 1) * band)
        res = blocks(view(q0, n0 * band), view(k0, n0 * band), view(v0, n0 * band), kp, vp,
                     first_has_no_prev=(n0 == 0))
        store(0, pl.ds(n0 * band, span), res)

    dil = ATT_DILATIONS[1]
    for n in range(seq // dil // band):
        cur = pl.ds(n * band, band)
        if n == 0:
            res = blocks(q1[:, cur, :], k1[:, cur, :], v1[:, cur, :])
        else:
            prev = pl.ds((n - 1) * band, band)
            res = blocks(q1[:, cur, :], k1[:, cur, :], v1[:, cur, :], k1[:, prev, :], v1[:, prev, :])
        for r in range(dil):
            store(1, pl.ds(r + dil * band * n, band, stride=dil), res, b=r)

    dil = ATT_DILATIONS[2]
    assert seq // dil == band and dil % nbatch == 0 and (seq // band) % nbatch == 0
    for r0 in range(0, dil, nbatch):
        cls = pl.ds(r0, nbatch)
        res = blocks(q2[cls], k2[cls], v2[cls])
        for b in range(nbatch):
            store(2, pl.ds(r0 + b, band, stride=dil), res, b=b)

    def merge_rows(i, carry):
        rows = pl.ds(pl.multiple_of(i * band, band), band)
        m0, m1, m2 = acc_m[0, rows, :], acc_m[1, rows, :], acc_m[2, rows, :]
        m = jnp.maximum(jnp.maximum(m0, m1), m2)
        w0, w1, w2 = jnp.exp(m0 - m), jnp.exp(m1 - m), jnp.exp(m2 - m)
        num = w0 * acc_o[0, rows, :] + w1 * acc_o[1, rows, :] + w2 * acc_o[2, rows, :]
        den = w0 * acc_l[0, rows, :] + w1 * acc_l[1, rows, :] + w2 * acc_l[2, rows, :]
        o_ref[rows, :] = (num / den).astype(o_ref.dtype)
        return carry

    lax.fori_loop(0, seq // band, merge_rows, 0)


def _attention(qkv, *, bsz, seq):
    n_pairs = ATT_GROUP_COLS // LANES
    in_specs = []
    for dil in ATT_DILATIONS:
        spec = pl.BlockSpec((None, dil, seq // dil, LANES), lambda b, p: (b, 0, 0, p))
        in_specs += [spec, spec, spec]
    return pl.pallas_call(
        functools.partial(_attn_kernel, seq=seq),
        grid=(bsz, n_pairs),
        in_specs=in_specs,
        out_specs=pl.BlockSpec((None, seq, LANES), lambda b, p: (b, 0, p)),
        out_shape=jax.ShapeDtypeStruct((bsz, seq, ATT_GROUP_COLS), BF16),
        scratch_shapes=[pltpu.VMEM((len(ATT_DILATIONS), seq, LANES), F32)] * 3,
        compiler_params=_cparams(2),
        name="dilated_attn",
    )(*qkv)


def _ssd_kernel(sz_ref, x_ref, bc_ref, dt_ref, alog_ref, dskip_ref, nw_ref, o_ref,
                st_scr, y_scr, cum_scr, cumt_scr, dtt_scr, swt_scr):
    L = SSD_CHUNK
    n_state = D_STATE
    d_inner = x_ref.shape[-1]
    n_heads = d_inner // SSM_HEAD_DIM
    heads_per_group = n_heads // N_SSM_GROUPS
    group_w = d_inner // N_SSM_GROUPS

    @pl.when(pl.program_id(1) == 0)
    def _():
        st_scr[...] = jnp.zeros_like(st_scr)

    dt = dt_ref[...]
    a = -jnp.exp(alog_ref[...])
    ti = lax.broadcasted_iota(jnp.int32, (L, L), 0)
    tj = lax.broadcasted_iota(jnp.int32, (L, L), 1)
    causal = ti >= tj
    cum = jnp.dot(causal.astype(F32), dt * a, preferred_element_type=F32,
                  precision=lax.Precision.HIGHEST)
    cum_t = cum.T
    dt_t = dt.T
    cum_scr[...] = cum
    cumt_scr[...] = cum_t
    dtt_scr[...] = dt_t
    swt_scr[...] = dt_t * jnp.exp(cum_t[:, L - 1:L] - cum_t)

    lane = lax.broadcasted_iota(jnp.int32, (L, LANES), 1)
    head0 = lane < SSM_HEAD_DIM

    for g in range(N_SSM_GROUPS):
        b_g = bc_ref[:, g * n_state:(g + 1) * n_state]
        c_g = bc_ref[:, (N_SSM_GROUPS + g) * n_state:(N_SSM_GROUPS + g + 1) * n_state]
        bt_g = b_g.astype(F32).T
        c_gf = c_g.astype(F32)
        cb = jnp.dot(c_g, bt_g.astype(BF16), preferred_element_type=F32)
        for pair in range(heads_per_group // 2):
            col = g * group_w + pair * LANES
            cols = slice(col, col + LANES)
            x_pair = x_ref[:, cols]
            x_pair_b = x_pair.astype(BF16)
            st_pair = st_scr[:, cols]
            rhs = jnp.concatenate([x_pair_b, st_pair.astype(BF16)], axis=0)
            ys, ds, es = [], [], []
            for h in (g * heads_per_group + 2 * pair, g * heads_per_group + 2 * pair + 1):
                colb = jnp.broadcast_to(cum_scr[:, h:h + 1], (L, L))
                rowb = cumt_scr[h:h + 1, :]
                dec = jnp.exp(jnp.where(causal, colb - rowb, -jnp.inf))
                m_h = (cb * dec * dtt_scr[h:h + 1, :]).astype(BF16)
                c_h = (c_gf * jnp.exp(colb)).astype(BF16)
                ys.append(jnp.dot(jnp.concatenate([m_h, c_h], axis=1), rhs,
                                  preferred_element_type=F32))
                bt_h = (bt_g * swt_scr[h:h + 1, :]).astype(BF16)
                ds.append(jnp.dot(bt_h, x_pair_b, preferred_element_type=F32))
                es.append(jnp.exp(cumt_scr[h:h + 1, L - 1:L]))
            st_scr[:, cols] = (st_pair * jnp.where(head0, es[0], es[1])
                               + jnp.where(head0, ds[0], ds[1]))
            y_scr[:, cols] = jnp.where(head0, ys[0], ys[1]) + dskip_ref[:, cols] * x_pair

    for g in range(N_SSM_GROUPS):
        cols = slice(g * group_w, (g + 1) * group_w)
        y = y_scr[:, cols] * sz_ref[:, cols]
        o_ref[:, cols] = _rms_norm(y, nw_ref[:, cols]).astype(o_ref.dtype)


def _ssd(zxg, bc, dt, alog, dskip, nw, *, bsz, seq, d_inner):
    L = SSD_CHUNK
    bc_w = bc.shape[-1]
    zxg3 = zxg.reshape(bsz, seq, zxg.shape[-1])

    def small(shape):
        return pl.BlockSpec(shape, lambda b, c: (0,) * len(shape))

    out = pl.pallas_call(
        _ssd_kernel,
        grid=(bsz, seq // L),
        in_specs=[pl.BlockSpec((None, L, d_inner), lambda b, c: (b, c, 0)),
                  pl.BlockSpec((None, L, d_inner), lambda b, c: (b, c, 1)),
                  pl.BlockSpec((None, L, bc_w), lambda b, c: (b, c, 0)),
                  pl.BlockSpec((None, L, LANES), lambda b, c: (b, c, 0)),
                  small((1, LANES)), small((1, d_inner)), small((1, d_inner))],
        out_specs=pl.BlockSpec((None, L, d_inner), lambda b, c: (b, c, 0)),
        out_shape=jax.ShapeDtypeStruct((bsz, seq, d_inner), BF16),
        scratch_shapes=[pltpu.VMEM((D_STATE, d_inner), F32),
                        pltpu.VMEM((L, d_inner), F32),
                        pltpu.VMEM((L, LANES), F32),
                        pltpu.VMEM((LANES, L), F32),
                        pltpu.VMEM((LANES, L), F32),
                        pltpu.VMEM((LANES, L), F32)],
        compiler_params=_cparams(2),
        name="ssd",
    )(zxg3, zxg3, bc.reshape(bsz, seq, bc_w), dt.reshape(bsz, seq, LANES), alog, dskip, nw)
    return out.reshape(bsz * seq, d_inner)


def _merge_kernel(x_ref, att_ref, yn_ref, gp_ref, bg_ref, wa_ref, ws_ref, wo_ref, o_ref):
    d = x_ref.shape[-1]
    ya = jnp.dot(att_ref[...], wa_ref[...], preferred_element_type=F32)
    ys = jnp.dot(yn_ref[...], ws_ref[...], preferred_element_type=F32)
    gates = _sigmoid(gp_ref[...] + bg_ref[...])
    mixed = (gates[:, :d] * ya + gates[:, d:] * ys).astype(BF16)
    o_ref[...] = x_ref[...] + jnp.dot(mixed, wo_ref[...], preferred_element_type=F32)


def _merge(x, att, yn, zxg, bg, wa, ws, wo, tm=512):
    t, d = x.shape
    d_inner = yn.shape[1]
    assert (2 * d_inner) % (2 * d) == 0
    gate_blk = (2 * d_inner) // (2 * d)
    return pl.pallas_call(
        _merge_kernel,
        grid=(t // tm,),
        in_specs=[pl.BlockSpec((tm, d), lambda i: (i, 0)),
                  pl.BlockSpec((tm, att.shape[1]), lambda i: (i, 0)),
                  pl.BlockSpec((tm, yn.shape[1]), lambda i: (i, 0)),
                  pl.BlockSpec((tm, 2 * d), lambda i: (i, gate_blk)),
                  _resident((1, 2 * d)), _resident(wa.shape), _resident(ws.shape), _resident(wo.shape)],
        out_specs=pl.BlockSpec((tm, d), lambda i: (i, 0)),
        out_shape=jax.ShapeDtypeStruct((t, d), F32),
        compiler_params=_cparams(1),
        name="merge",
    )(x, att, yn, zxg, bg, wa, ws, wo)


def _rope_tables(seq):
    pos = jnp.arange(seq, dtype=F32)
    inv_freq = 1.0 / (ROPE_THETA ** (jnp.arange(0, HEAD_DIM, 2, dtype=F32) / HEAD_DIM))
    ang = pos[:, None] * inv_freq[None, :]
    cos, sin = jnp.cos(ang), jnp.sin(ang)
    reps = LANES // HEAD_DIM
    return (jnp.tile(jnp.concatenate([cos, cos], axis=1), (1, reps)),
            jnp.tile(jnp.concatenate([-sin, sin], axis=1), (1, reps)))


def kernel(x, ffn1_norm_w, ffn1_w_gate, ffn1_w_up, ffn1_w_down, mix_norm_w, w_in, b_gates,
           q_norm_w, k_norm_w, conv_w, conv_b, dt_bias, a_log, d_skip, ssm_norm_w,
           w_att_proj, w_ssm_proj, w_out, ffn2_norm_w, ffn2_w_gate, ffn2_w_up, ffn2_w_down):
    bsz, seq, d_model = x.shape
    depth = w_in.shape[0]
    d_inner = ssm_norm_w.shape[1]
    n_ssm_heads = dt_bias.shape[1]
    qkv_cols = len(ATT_DILATIONS) * ATT_GROUP_COLS
    bc_w = 2 * N_SSM_GROUPS * D_STATE
    assert seq % (ATT_DILATIONS[-1] * ATT_BAND) == 0 and seq % 1024 == 0
    assert w_in.shape[2] == 3 * qkv_cols + d_inner + (d_inner + bc_w) + n_ssm_heads + 2 * d_model
    assert n_ssm_heads <= LANES

    o_z = 3 * qkv_cols
    o_x = o_z + d_inner
    o_bc = o_x + d_inner
    o_dt = o_bc + bc_w
    o_g = o_dt + n_ssm_heads
    gc = ATT_GROUP_COLS
    w_groups = [jnp.concatenate([w_in[:, :, s * qkv_cols + g * gc: s * qkv_cols + (g + 1) * gc]
                                 for s in range(3)], axis=2).astype(BF16)
                for g in range(len(ATT_DILATIONS))]
    w_main = jnp.concatenate([w_in[:, :, o_z:o_bc], w_in[:, :, o_g:], w_in[:, :, o_bc:o_dt]],
                             axis=2).astype(BF16)
    pad_h = LANES - n_ssm_heads
    w_dt = jnp.pad(w_in[:, :, o_dt:o_g], ((0, 0), (0, 0), (0, pad_h))).astype(BF16)
    zeros_zg = jnp.zeros((depth, CONV_WIDTH, d_inner), F32)
    conv_w_m = jnp.concatenate([zeros_zg, conv_w[:, :, :d_inner], jnp.zeros((depth, CONV_WIDTH, 2 * d_model), F32),
                                conv_w[:, :, d_inner:]], axis=2)
    conv_b_m = jnp.concatenate([zeros_zg[:, :1], conv_b[:, None, :d_inner],
                                jnp.zeros((depth, 1, 2 * d_model), F32), conv_b[:, None, d_inner:]], axis=2)
    f1g, f1u, f1d = ffn1_w_gate.astype(BF16), ffn1_w_up.astype(BF16), ffn1_w_down.astype(BF16)
    f2g, f2u, f2d = ffn2_w_gate.astype(BF16), ffn2_w_up.astype(BF16), ffn2_w_down.astype(BF16)
    wa_b, ws_b, wo_b = w_att_proj.astype(BF16), w_ssm_proj.astype(BF16), w_out.astype(BF16)

    reps = LANES // HEAD_DIM
    head_nw = jnp.stack([jnp.tile(q_norm_w * (HEAD_DIM ** -0.5), (1, reps)),
                         jnp.tile(k_norm_w, (1, reps))], axis=1)[:, :, None, :]
    cos_t, sin_t = _rope_tables(seq)
    hid = jnp.arange(LANES) // HEAD_DIM
    block_diag = jnp.where(hid[:, None] == hid[None, :], 1.0 / HEAD_DIM, 0.0).astype(BF16)
    dtb_p = jnp.pad(dt_bias, ((0, 0), (0, pad_h)))[:, None, :]
    alog_p = jnp.pad(a_log, ((0, 0), (0, pad_h)))[:, None, :]
    dskip_e = jnp.repeat(d_skip, SSM_HEAD_DIM, axis=1)[:, None, :]

    xt = x.reshape(bsz * seq, d_model)
    for i in range(depth):
        xt, h = _ffn(xt, ffn1_norm_w[i][None], f1g[i], f1u[i], f1d[i], mix_norm_w[i][None])
        qkv = []
        for g, dil in enumerate(ATT_DILATIONS):
            qkv += _qkv(h, w_groups[g][i], head_nw[i], cos_t, sin_t, block_diag,
                        bsz=bsz, seq=seq, dil=dil)
        zxg, bc, dt = _proj(h, w_main[i], w_dt[i], conv_w_m[i], conv_b_m[i], dtb_p[i],
                            seq=seq, d_inner=d_inner, d_model=d_model)
        att = _attention(qkv, bsz=bsz, seq=seq).reshape(bsz * seq, ATT_GROUP_COLS)
        yn = _ssd(zxg, bc, dt, alog_p[i], dskip_e[i], ssm_norm_w[i][None],
                  bsz=bsz, seq=seq, d_inner=d_inner)
        xt = _merge(xt, att, yn, zxg, b_gates[i][None], wa_b[i], ws_b[i], wo_b[i])
        xt = _ffn(xt, ffn2_norm_w[i][None], f2g[i], f2u[i], f2d[i])
    return xt.reshape(bsz, seq, d_model)
```

```python
import functools

import jax
import jax.numpy as jnp
from jax import lax
from jax.experimental import pallas as pl
from jax.experimental.pallas import tpu as pltpu

F32 = jnp.float32
BF16 = jnp.bfloat16

NORM_EPS = 1e-6
ROPE_THETA = 10000.0
HEAD_DIM = 64
ATT_DILATIONS = (1, 4, 16)
ATT_BAND = 128
DEINTERLEAVE_STRIDE = 4
ATT_BLOCK_BATCH = 8
HEADS_PER_GROUP = 8
ATT_GROUP_COLS = HEADS_PER_GROUP * HEAD_DIM
SSM_HEAD_DIM = 64
N_SSM_GROUPS = 4
D_STATE = 128
CONV_WIDTH = 4
SSD_CHUNK = 128
SSD_CHUNKS_PER_STEP = 4
LANES = 128
CONV_CARRY_ROWS = 8
PROJ_CONV_ROWS = 256
PROJ_TN = 1024
VMEM_LIMIT = 56 * 1024 * 1024


def _cparams(n_axes):
    return pltpu.CompilerParams(dimension_semantics=("arbitrary",) * n_axes,
                                vmem_limit_bytes=VMEM_LIMIT)


def _rms_norm(x, w):
    return x * lax.rsqrt(jnp.mean(x * x, axis=-1, keepdims=True) + NORM_EPS) * w


def _sigmoid(x):
    return 0.5 * jnp.tanh(0.5 * x) + 0.5


def _silu(x):
    h = 0.5 * x
    return h * jnp.tanh(h) + h


def _softplus(x):
    return jnp.maximum(x, 0.0) + jnp.log1p(jnp.exp(-jnp.abs(x)))


def _resident(shape):
    return pl.BlockSpec(shape, lambda *_: (0,) * len(shape), pipeline_mode=pl.Buffered(1))


def _ffn_kernel(x_ref, nw_ref, wg_ref, wu_ref, wd_ref, *rest):
    x = x_ref[...]
    h = _rms_norm(x, nw_ref[...]).astype(BF16)
    g = jnp.dot(h, wg_ref[...], preferred_element_type=F32)
    u = jnp.dot(h, wu_ref[...], preferred_element_type=F32)
    a = (_silu(g) * u).astype(BF16)
    y = jnp.dot(a, wd_ref[...], preferred_element_type=F32)
    out = x + 0.5 * y
    if len(rest) == 1:
        rest[0][...] = out
    else:
        nw_next_ref, o_ref, h_ref = rest
        o_ref[...] = out
        h_ref[...] = _rms_norm(out, nw_next_ref[...]).astype(BF16)


def _ffn(x, nw, wg, wu, wd, nw_next=None, tm=512):
    t, d = x.shape
    f = wg.shape[1]
    row = pl.BlockSpec((tm, d), lambda i: (i, 0))
    in_specs = [row, _resident((1, d)), _resident((d, f)), _resident((d, f)), _resident((f, d))]
    args = [x, nw, wg, wu, wd]
    out_specs, out_shape = row, jax.ShapeDtypeStruct((t, d), F32)
    if nw_next is not None:
        in_specs.append(_resident((1, d)))
        args.append(nw_next)
        out_specs, out_shape = [row, row], [out_shape, jax.ShapeDtypeStruct((t, d), BF16)]
    return pl.pallas_call(
        _ffn_kernel,
        grid=(t // tm,),
        in_specs=in_specs,
        out_specs=out_specs,
        out_shape=out_shape,
        compiler_params=_cparams(1),
        name="ffn",
    )(*args)


def _proj_kernel(h_ref, w_ref, wdt_ref, cw_ref, cb_ref, dtb_ref, x_ref, sz_ref, g_ref, bc_ref, dt_ref,
                 conv_scr, conv_scr2, hist_scr, *, tm, tn, tiles_per_seq):
    i = pl.program_id(0)
    j = pl.program_id(1)
    hist = CONV_CARRY_ROWS
    sub = PROJ_CONV_ROWS
    n_sub = tm // sub
    bufs = (conv_scr, conv_scr2)

    def matmul(s, half):
        return jnp.dot(h_ref[pl.ds(s * sub, sub), :], w_ref[:, half * tn:(half + 1) * tn],
                       preferred_element_type=F32)

    def conv_start(slot):
        @pl.when(i % tiles_per_seq == 0)
        def _():
            conv_scr[pl.ds(0, hist), :] = jnp.zeros((hist, tn), F32)

        @pl.when(i % tiles_per_seq != 0)
        def _():
            conv_scr[pl.ds(0, hist), :] = hist_scr[slot]

    def conv_rows(slot, s, out_ref):
        buf = bufs[s % 2]
        buf[pl.ds(hist, sub), :] = matmul(s, 0)
        rows = buf[...]
        acc = cb_ref[...] + cw_ref[CONV_WIDTH - 1:CONV_WIDTH, :] * rows[hist:]
        for k in range(1, CONV_WIDTH):
            acc = acc + (cw_ref[CONV_WIDTH - 1 - k:CONV_WIDTH - k, :]
                         * pltpu.roll(rows, k, 0)[hist:])
        out_ref[pl.ds(s * sub, sub), :] = _silu(acc).astype(out_ref.dtype)
        tail = buf[pl.ds(sub, hist), :]
        if s + 1 < n_sub:
            bufs[(s + 1) % 2][pl.ds(0, hist), :] = tail
        else:
            hist_scr[slot] = tail

    for jj in range(2):
        @pl.when(j == jj)
        def _(jj=jj):
            conv_start(jj)
            for s in range(n_sub):
                conv_rows(jj, s, x_ref)
                sz_ref[pl.ds(s * sub, sub), :] = _silu(matmul(s, 1))

    @pl.when(j == 2)
    def _():
        conv_start(2)
        for s in range(n_sub):
            conv_rows(2, s, bc_ref)
            g_ref[pl.ds(s * sub, sub), :] = matmul(s, 1)

    @pl.when(j == 3)
    def _():
        for s in range(n_sub):
            g_ref[pl.ds(s * sub, sub), :] = matmul(s, 0)
        dt_raw = jnp.dot(h_ref[...], wdt_ref[...], preferred_element_type=F32)
        dt_ref[...] = _softplus(dt_raw + dtb_ref[...])


def _proj(h, w_pairs, w_dt, conv_w, conv_b, dt_bias, *, seq, d_inner, d_model, tm=1024, tn=PROJ_TN):
    t, d = h.shape
    bc_w = 2 * N_SSM_GROUPS * D_STATE
    assert bc_w == tn and d_inner == 2 * tn and 2 * d_model == 2 * tn and w_pairs.shape[1] == 8 * tn
    f32_out = jax.ShapeDtypeStruct((t, 2 * tn), F32)
    return pl.pallas_call(
        functools.partial(_proj_kernel, tm=tm, tn=tn, tiles_per_seq=seq // tm),
        grid=(t // tm, 4),
        in_specs=[pl.BlockSpec((tm, d), lambda i, j: (i, 0)),
                  pl.BlockSpec((d, 2 * tn), lambda i, j: (0, j)),
                  pl.BlockSpec((d, LANES), lambda i, j: (0, 0)),
                  pl.BlockSpec((CONV_WIDTH, tn), lambda i, j: (0, jnp.minimum(j, 2))),
                  pl.BlockSpec((1, tn), lambda i, j: (0, jnp.minimum(j, 2))),
                  pl.BlockSpec((1, LANES), lambda i, j: (0, 0))],
        out_specs=[pl.BlockSpec((tm, tn), lambda i, j: (i, jnp.minimum(j, 1))),
                   pl.BlockSpec((tm, tn), lambda i, j: (i, jnp.minimum(j, 1))),
                   pl.BlockSpec((tm, tn), lambda i, j: (i, jnp.maximum(j, 2) - 2)),
                   pl.BlockSpec((tm, tn), lambda i, j: (i, 0)),
                   pl.BlockSpec((tm, LANES), lambda i, j: (i, 0))],
        out_shape=[f32_out, f32_out, f32_out,
                   jax.ShapeDtypeStruct((t, bc_w), BF16),
                   jax.ShapeDtypeStruct((t, LANES), F32)],
        scratch_shapes=[pltpu.VMEM((PROJ_CONV_ROWS + CONV_CARRY_ROWS, tn), F32),
                        pltpu.VMEM((PROJ_CONV_ROWS + CONV_CARRY_ROWS, tn), F32),
                        pltpu.VMEM((3, CONV_CARRY_ROWS, tn), F32)],
        compiler_params=_cparams(2),
        name="proj_zxg",
    )(h, w_pairs, w_dt, conv_w, conv_b, dt_bias)


def _qkv_kernel(h_ref, w_ref, hnw_ref, cos_ref, sin_ref, bd_ref,
                q_ref, k_ref, v_ref, res_scr, tmp_scr, *, dil, tm):
    n = tm // dil
    gc = ATT_GROUP_COLS
    n_chunks = gc // LANES
    h = h_ref[...]
    lane = lax.broadcasted_iota(jnp.int32, (tm, LANES), 1)
    first_half = (lane % HEAD_DIM) < (HEAD_DIM // 2)

    def store_classes(part, out_ref):
        for c in range(n_chunks):
            cols = slice(c * LANES, (c + 1) * LANES)
            src = res_scr.at[part, c]
            if dil == 1:
                out_ref[0, :, cols] = src[...].astype(BF16)
            elif dil <= DEINTERLEAVE_STRIDE:
                for r in range(dil):
                    out_ref[r, :, cols] = src[pl.ds(r, n, stride=dil), :].astype(BF16)
            else:
                s1 = DEINTERLEAVE_STRIDE
                s2 = dil // s1
                n1 = tm // s1
                tmp = tmp_scr.at[part]
                for r1 in range(s1):
                    tmp[pl.ds(r1 * n1, n1), :] = src[pl.ds(r1, n1, stride=s1), :]
                for r1 in range(s1):
                    for r2 in range(s2):
                        out_ref[r1 + s1 * r2, :, cols] = (
                            tmp[pl.ds(r1 * n1 + r2, n, stride=s2), :].astype(BF16))

    for part, out_ref in enumerate((q_ref, k_ref, v_ref)):
        res = jnp.dot(h, w_ref[:, part * gc:(part + 1) * gc], preferred_element_type=F32)
        for c in range(n_chunks):
            xr = res[:, c * LANES:(c + 1) * LANES]
            if part < 2:
                x2 = xr * xr
                hi = x2.astype(BF16)
                lo = (x2 - hi.astype(F32)).astype(BF16)
                ms = (jnp.dot(hi, bd_ref[...], preferred_element_type=F32)
                      + jnp.dot(lo, bd_ref[...], preferred_element_type=F32))
                y = xr * lax.rsqrt(ms + NORM_EPS) * hnw_ref[part]
                partner = jnp.where(first_half,
                                    pltpu.roll(y, LANES - HEAD_DIM // 2, 1),
                                    pltpu.roll(y, HEAD_DIM // 2, 1))
                xr = y * cos_ref[...] + partner * sin_ref[...]
            res_scr[part, c] = xr
        store_classes(part, out_ref)


def _qkv(h, w, hnw, cos, sin, bd, *, bsz, seq, dil, tm=1024):
    t, d = h.shape
    tiles_per_seq = seq // tm
    n = tm // dil
    gc = ATT_GROUP_COLS
    assert dil <= DEINTERLEAVE_STRIDE or dil % DEINTERLEAVE_STRIDE == 0
    out_sds = jax.ShapeDtypeStruct((bsz, dil, seq // dil, gc), BF16)
    out_spec = pl.BlockSpec((None, dil, n, gc), lambda i: (i // tiles_per_seq, 0, i % tiles_per_seq, 0))
    return pl.pallas_call(
        functools.partial(_qkv_kernel, dil=dil, tm=tm),
        grid=(t // tm,),
        in_specs=[pl.BlockSpec((tm, d), lambda i: (i, 0)),
                  _resident((d, 3 * gc)),
                  _resident((2, 1, LANES)),
                  pl.BlockSpec((tm, LANES), lambda i: (i % tiles_per_seq, 0)),
                  pl.BlockSpec((tm, LANES), lambda i: (i % tiles_per_seq, 0)),
                  _resident((LANES, LANES))],
        out_specs=[out_spec, out_spec, out_spec],
        out_shape=[out_sds, out_sds, out_sds],
        scratch_shapes=[pltpu.VMEM((3, gc // LANES, tm, LANES), F32), pltpu.VMEM((3, tm, LANES), F32)],
        compiler_params=_cparams(1),
        name=f"qkv_d{dil}",
    )(h, w, hnw, cos, sin, bd)


def _attn_kernel(q0, k0, v0, q1, k1, v1, q2, k2, v2, o_ref, acc_o, acc_m, acc_l, *, seq):
    band = ATT_BAND
    nbatch = ATT_BLOCK_BATCH
    lane = lax.broadcasted_iota(jnp.int32, (band, LANES), 1)
    head0 = lane < HEAD_DIM
    iq = lax.broadcasted_iota(jnp.int32, (band, band), 0)
    ik = lax.broadcasted_iota(jnp.int32, (band, band), 1)
    cur_mask = ik <= iq
    prev_mask = ik >= iq
    nt_dims = (((2,), (2,)), ((0,), (0,)))
    nn_dims = (((2,), (1,)), ((0,), (0,)))

    cur_mask2 = jnp.concatenate([cur_mask, cur_mask], axis=0)
    prev_mask2 = jnp.concatenate([prev_mask, prev_mask], axis=0)

    def blocks(qb, kc, vc, kp=None, vp=None, first_has_no_prev=False):
        zero = jnp.zeros_like(qb)
        ones = jnp.ones(vc.shape, BF16)
        qs = jnp.concatenate([jnp.where(head0, qb, zero), jnp.where(head0, zero, qb)], axis=1)
        sc = lax.dot_general(qs, kc, nt_dims, preferred_element_type=F32)
        sc = jnp.where(cur_mask2, sc, -jnp.inf)
        m = jnp.max(sc, axis=-1, keepdims=True)
        if kp is not None:
            sp = lax.dot_general(qs, kp, nt_dims, preferred_element_type=F32)
            sp = jnp.where(prev_mask2, sp, -jnp.inf)
            if first_has_no_prev:
                bidx = lax.broadcasted_iota(jnp.int32, sp.shape, 0)
                sp = jnp.where(bidx > 0, sp, -jnp.inf)
            m = jnp.maximum(m, jnp.max(sp, axis=-1, keepdims=True))
        acc = lax.dot_general(jnp.exp(sc - m).astype(BF16), jnp.concatenate([vc, ones], axis=-1),
                              nn_dims, preferred_element_type=F32)
        if kp is not None:
            acc = acc + lax.dot_general(jnp.exp(sp - m).astype(BF16),
                                        jnp.concatenate([vp, ones], axis=-1),
                                        nn_dims, preferred_element_type=F32)
        top, bot = acc[:, :band], acc[:, band:]
        return (jnp.where(head0, top[..., :LANES], bot[..., :LANES]),
                jnp.where(head0, m[:, :band], m[:, band:]),
                jnp.where(head0, top[..., LANES:], bot[..., LANES:]))

    def store(g, rows, res, b=None):
        for acc, val in zip((acc_o, acc_m, acc_l), res):
            acc[g, rows, :] = val.reshape(-1, LANES) if b is None else val[b]

    span = nbatch * band
    for n0 in range(0, seq // band, nbatch):
        def view(ref, start):
            return ref[0, pl.ds(start, span), :].reshape(nbatch, band, LANES)
        if n0 == 0:
            def shifted(ref):
                return jnp.concatenate([ref[0, pl.ds(0, band), :], ref[0, pl.ds(0, span - band), :]],
                                       axis=0).reshape(nbatch, band, LANES)
            kp, vp = shifted(k0), shifted(v0)
        else:
            kp, vp = view(k0, (n0 - 1) * band), view(v0, (n0 - 1) * band)
        res = blocks(view(q0, n0 * band), view(k0, n0 * band), view(v0, n0 * band), kp, vp,
                     first_has_no_prev=(n0 == 0))
        store(0, pl.ds(n0 * band, span), res)

    dil = ATT_DILATIONS[1]
    for n in range(seq // dil // band):
        cur = pl.ds(n * band, band)
        if n == 0:
            res = blocks(q1[:, cur, :], k1[:, cur, :], v1[:, cur, :])
        else:
            prev = pl.ds((n - 1) * band, band)
            res = blocks(q1[:, cur, :], k1[:, cur, :], v1[:, cur, :], k1[:, prev, :], v1[:, prev, :])
        for r in range(dil):
            store(1, pl.ds(r + dil * band * n, band, stride=dil), res, b=r)

    dil = ATT_DILATIONS[2]
    assert seq // dil == band and dil % nbatch == 0 and (seq // band) % nbatch == 0
    for r0 in range(0, dil, nbatch):
        cls = pl.ds(r0, nbatch)
        res = blocks(q2[cls], k2[cls], v2[cls])
        for b in range(nbatch):
            store(2, pl.ds(r0 + b, band, stride=dil), res, b=b)

    def merge_rows(i, carry):
        rows = pl.ds(pl.multiple_of(i * band, band), band)
        m0, m1, m2 = acc_m[0, rows, :], acc_m[1, rows, :], acc_m[2, rows, :]
        m = jnp.maximum(jnp.maximum(m0, m1), m2)
        w0, w1, w2 = jnp.exp(m0 - m), jnp.exp(m1 - m), jnp.exp(m2 - m)
        num = w0 * acc_o[0, rows, :] + w1 * acc_o[1, rows, :] + w2 * acc_o[2, rows, :]
        den = w0 * acc_l[0, rows, :] + w1 * acc_l[1, rows, :] + w2 * acc_l[2, rows, :]
        o_ref[rows, :] = (num / den).astype(o_ref.dtype)
        return carry

    lax.fori_loop(0, seq // band, merge_rows, 0)


def _attention(qkv, *, bsz, seq):
    n_pairs = ATT_GROUP_COLS // LANES
    in_specs = []
    for dil in ATT_DILATIONS:
        spec = pl.BlockSpec((None, dil, seq // dil, LANES), lambda b, p: (b, 0, 0, p))
        in_specs += [spec, spec, spec]
    return pl.pallas_call(
        functools.partial(_attn_kernel, seq=seq),
        grid=(bsz, n_pairs),
        in_specs=in_specs,
        out_specs=pl.BlockSpec((None, seq, LANES), lambda b, p: (b, 0, p)),
        out_shape=jax.ShapeDtypeStruct((bsz, seq, ATT_GROUP_COLS), BF16),
        scratch_shapes=[pltpu.VMEM((len(ATT_DILATIONS), seq, LANES), F32)] * 3,
        compiler_params=_cparams(2),
        name="dilated_attn",
    )(*qkv)


def _ssd_kernel(sz_ref, x_ref, bc_ref, dt_ref, alog_ref, dskip_ref, nw_ref, o_ref,
                st_scr, y_scr, cum_scr, cumt_scr, dtt_scr, swt_scr):
    @pl.when(pl.program_id(1) == 0)
    def _():
        st_scr[...] = jnp.zeros_like(st_scr)

    for ci in range(SSD_CHUNKS_PER_STEP):
        _ssd_chunk(pl.ds(ci * SSD_CHUNK, SSD_CHUNK), sz_ref, x_ref, bc_ref, dt_ref, alog_ref, dskip_ref,
                   nw_ref, o_ref, st_scr,
                   *(r.at[ci] for r in (y_scr, cum_scr, cumt_scr, dtt_scr, swt_scr)))


def _ssd_chunk(rows, sz_ref, x_ref, bc_ref, dt_ref, alog_ref, dskip_ref, nw_ref, o_ref,
               st_scr, y_scr, cum_scr, cumt_scr, dtt_scr, swt_scr):
    L = SSD_CHUNK
    n_state = D_STATE
    d_inner = x_ref.shape[-1]
    n_heads = d_inner // SSM_HEAD_DIM
    heads_per_group = n_heads // N_SSM_GROUPS
    group_w = d_inner // N_SSM_GROUPS

    dt = dt_ref[rows, :]
    a = -jnp.exp(alog_ref[...])
    ti = lax.broadcasted_iota(jnp.int32, (L, L), 0)
    tj = lax.broadcasted_iota(jnp.int32, (L, L), 1)
    causal = ti >= tj
    cum = jnp.dot(causal.astype(F32), dt * a, preferred_element_type=F32,
                  precision=lax.Precision.HIGHEST)
    cum_t = cum.T
    dt_t = dt.T
    cum_scr[...] = cum
    cumt_scr[...] = cum_t
    dtt_scr[...] = dt_t
    swt_scr[...] = dt_t * jnp.exp(cum_t[:, L - 1:L] - cum_t)

    lane = lax.broadcasted_iota(jnp.int32, (L, LANES), 1)
    head0 = lane < SSM_HEAD_DIM

    for g in range(N_SSM_GROUPS):
        b_g = bc_ref[rows, g * n_state:(g + 1) * n_state]
        c_g = bc_ref[rows, (N_SSM_GROUPS + g) * n_state:(N_SSM_GROUPS + g + 1) * n_state]
        bt_g = b_g.astype(F32).T
        c_gf = c_g.astype(F32)
        cb = jnp.dot(c_g, bt_g.astype(BF16), preferred_element_type=F32)
        for pair in range(heads_per_group // 2):
            col = g * group_w + pair * LANES
            cols = slice(col, col + LANES)
            x_pair = x_ref[rows, cols]
            x_pair_b = x_pair.astype(BF16)
            st_pair = st_scr[:, cols]
            rhs = jnp.concatenate([x_pair_b, st_pair.astype(BF16)], axis=0)
            ys, ds, es = [], [], []
            for h in (g * heads_per_group + 2 * pair, g * heads_per_group + 2 * pair + 1):
                colb = jnp.broadcast_to(cum_scr[:, h:h + 1], (L, L))
                rowb = cumt_scr[h:h + 1, :]
                dec = jnp.exp(jnp.where(causal, colb - rowb, -jnp.inf))
                m_h = (cb * dec * dtt_scr[h:h + 1, :]).astype(BF16)
                c_h = (c_gf * jnp.exp(colb)).astype(BF16)
                ys.append(jnp.dot(jnp.concatenate([m_h, c_h], axis=1), rhs,
                                  preferred_element_type=F32))
                bt_h = (bt_g * swt_scr[h:h + 1, :]).astype(BF16)
                ds.append(jnp.dot(bt_h, x_pair_b, preferred_element_type=F32))
                es.append(jnp.exp(cumt_scr[h:h + 1, L - 1:L]))
            st_scr[:, cols] = (st_pair * jnp.where(head0, es[0], es[1])
                               + jnp.where(head0, ds[0], ds[1]))
            y_scr[:, cols] = jnp.where(head0, ys[0], ys[1]) + dskip_ref[:, cols] * x_pair

    for g in range(N_SSM_GROUPS):
        cols = slice(g * group_w, (g + 1) * group_w)
        y = y_scr[:, cols] * sz_ref[rows, cols]
        o_ref[rows, cols] = _rms_norm(y, nw_ref[:, cols]).astype(o_ref.dtype)


def _ssd(sz, x, bc, dt, alog, dskip, nw, *, bsz, seq, d_inner):
    L = SSD_CHUNK * SSD_CHUNKS_PER_STEP
    nck = SSD_CHUNKS_PER_STEP
    bc_w = bc.shape[-1]

    def small(shape):
        return pl.BlockSpec(shape, lambda b, c: (0,) * len(shape))

    out = pl.pallas_call(
        _ssd_kernel,
        grid=(bsz, seq // L),
        in_specs=[pl.BlockSpec((None, L, d_inner), lambda b, c: (b, c, 0)),
                  pl.BlockSpec((None, L, d_inner), lambda b, c: (b, c, 0)),
                  pl.BlockSpec((None, L, bc_w), lambda b, c: (b, c, 0)),
                  pl.BlockSpec((None, L, LANES), lambda b, c: (b, c, 0)),
                  small((1, LANES)), small((1, d_inner)), small((1, d_inner))],
        out_specs=pl.BlockSpec((None, L, d_inner), lambda b, c: (b, c, 0)),
        out_shape=jax.ShapeDtypeStruct((bsz, seq, d_inner), BF16),
        scratch_shapes=[pltpu.VMEM((D_STATE, d_inner), F32),
                        pltpu.VMEM((nck, SSD_CHUNK, d_inner), F32),
                        pltpu.VMEM((nck, SSD_CHUNK, LANES), F32),
                        pltpu.VMEM((nck, LANES, SSD_CHUNK), F32),
                        pltpu.VMEM((nck, LANES, SSD_CHUNK), F32),
                        pltpu.VMEM((nck, LANES, SSD_CHUNK), F32)],
        compiler_params=_cparams(2),
        name="ssd",
    )(sz.reshape(bsz, seq, d_inner), x.reshape(bsz, seq, d_inner), bc.reshape(bsz, seq, bc_w),
      dt.reshape(bsz, seq, LANES), alog, dskip, nw)
    return out.reshape(bsz * seq, d_inner)


def _merge_kernel(x_ref, att_ref, yn_ref, gp_ref, bg_ref, wa_ref, ws_ref, wo_ref, o_ref):
    d = x_ref.shape[-1]
    ya = jnp.dot(att_ref[...], wa_ref[...], preferred_element_type=F32)
    ys = jnp.dot(yn_ref[...], ws_ref[...], preferred_element_type=F32)
    gates = _sigmoid(gp_ref[...] + bg_ref[...])
    mixed = (gates[:, :d] * ya + gates[:, d:] * ys).astype(BF16)
    o_ref[...] = x_ref[...] + jnp.dot(mixed, wo_ref[...], preferred_element_type=F32)


def _merge(x, att, yn, gates, bg, wa, ws, wo, tm=512):
    t, d = x.shape
    return pl.pallas_call(
        _merge_kernel,
        grid=(t // tm,),
        in_specs=[pl.BlockSpec((tm, d), lambda i: (i, 0)),
                  pl.BlockSpec((tm, att.shape[1]), lambda i: (i, 0)),
                  pl.BlockSpec((tm, yn.shape[1]), lambda i: (i, 0)),
                  pl.BlockSpec((tm, 2 * d), lambda i: (i, 0)),
                  _resident((1, 2 * d)), _resident(wa.shape), _resident(ws.shape), _resident(wo.shape)],
        out_specs=pl.BlockSpec((tm, d), lambda i: (i, 0)),
        out_shape=jax.ShapeDtypeStruct((t, d), F32),
        compiler_params=_cparams(1),
        name="merge",
    )(x, att, yn, gates, bg, wa, ws, wo)


def _rope_tables(seq):
    pos = jnp.arange(seq, dtype=F32)
    inv_freq = 1.0 / (ROPE_THETA ** (jnp.arange(0, HEAD_DIM, 2, dtype=F32) / HEAD_DIM))
    ang = pos[:, None] * inv_freq[None, :]
    cos, sin = jnp.cos(ang), jnp.sin(ang)
    reps = LANES // HEAD_DIM
    return (jnp.tile(jnp.concatenate([cos, cos], axis=1), (1, reps)),
            jnp.tile(jnp.concatenate([-sin, sin], axis=1), (1, reps)))


def kernel(x, ffn1_norm_w, ffn1_w_gate, ffn1_w_up, ffn1_w_down, mix_norm_w, w_in, b_gates,
           q_norm_w, k_norm_w, conv_w, conv_b, dt_bias, a_log, d_skip, ssm_norm_w,
           w_att_proj, w_ssm_proj, w_out, ffn2_norm_w, ffn2_w_gate, ffn2_w_up, ffn2_w_down):
    bsz, seq, d_model = x.shape
    depth = w_in.shape[0]
    d_inner = ssm_norm_w.shape[1]
    n_ssm_heads = dt_bias.shape[1]
    qkv_cols = len(ATT_DILATIONS) * ATT_GROUP_COLS
    bc_w = 2 * N_SSM_GROUPS * D_STATE
    assert seq % (ATT_DILATIONS[-1] * ATT_BAND) == 0 and seq % 1024 == 0
    assert seq % (SSD_CHUNK * SSD_CHUNKS_PER_STEP) == 0
    assert w_in.shape[2] == 3 * qkv_cols + d_inner + (d_inner + bc_w) + n_ssm_heads + 2 * d_model
    assert n_ssm_heads <= LANES

    o_z = 3 * qkv_cols
    o_x = o_z + d_inner
    o_bc = o_x + d_inner
    o_dt = o_bc + bc_w
    o_g = o_dt + n_ssm_heads
    gc = ATT_GROUP_COLS
    w_groups = [jnp.concatenate([w_in[:, :, s * qkv_cols + g * gc: s * qkv_cols + (g + 1) * gc]
                                 for s in range(3)], axis=2).astype(BF16)
                for g in range(len(ATT_DILATIONS))]
    tn = PROJ_TN
    assert d_inner == 2 * tn and d_model == tn and bc_w == tn
    z_w, x_w, bc_wt, g_w = w_in[:, :, o_z:o_x], w_in[:, :, o_x:o_bc], w_in[:, :, o_bc:o_dt], w_in[:, :, o_g:]
    w_pairs = jnp.concatenate([x_w[:, :, :tn], z_w[:, :, :tn], x_w[:, :, tn:], z_w[:, :, tn:],
                               bc_wt, g_w[:, :, :tn], g_w[:, :, tn:], jnp.zeros_like(bc_wt)],
                              axis=2).astype(BF16)
    pad_h = LANES - n_ssm_heads
    w_dt = jnp.pad(w_in[:, :, o_dt:o_g], ((0, 0), (0, 0), (0, pad_h))).astype(BF16)
    f1g, f1u, f1d = ffn1_w_gate.astype(BF16), ffn1_w_up.astype(BF16), ffn1_w_down.astype(BF16)
    f2g, f2u, f2d = ffn2_w_gate.astype(BF16), ffn2_w_up.astype(BF16), ffn2_w_down.astype(BF16)
    wa_b, ws_b, wo_b = w_att_proj.astype(BF16), w_ssm_proj.astype(BF16), w_out.astype(BF16)

    reps = LANES // HEAD_DIM
    head_nw = jnp.stack([jnp.tile(q_norm_w * (HEAD_DIM ** -0.5), (1, reps)),
                         jnp.tile(k_norm_w, (1, reps))], axis=1)[:, :, None, :]
    cos_t, sin_t = _rope_tables(seq)
    hid = jnp.arange(LANES) // HEAD_DIM
    block_diag = jnp.where(hid[:, None] == hid[None, :], 1.0 / HEAD_DIM, 0.0).astype(BF16)
    dtb_p = jnp.pad(dt_bias, ((0, 0), (0, pad_h)))[:, None, :]
    alog_p = jnp.pad(a_log, ((0, 0), (0, pad_h)))[:, None, :]
    dskip_e = jnp.repeat(d_skip, SSM_HEAD_DIM, axis=1)[:, None, :]

    xt = x.reshape(bsz * seq, d_model)
    for i in range(depth):
        xt, h = _ffn(xt, ffn1_norm_w[i][None], f1g[i], f1u[i], f1d[i], mix_norm_w[i][None])
        qkv = []
        for g, dil in enumerate(ATT_DILATIONS):
            qkv += _qkv(h, w_groups[g][i], head_nw[i], cos_t, sin_t, block_diag,
                        bsz=bsz, seq=seq, dil=dil)
        xc, sz, gates, bc, dt = _proj(h, w_pairs[i], w_dt[i], conv_w[i], conv_b[i][None], dtb_p[i],
                                      seq=seq, d_inner=d_inner, d_model=d_model)
        att = _attention(qkv, bsz=bsz, seq=seq).reshape(bsz * seq, ATT_GROUP_COLS)
        yn = _ssd(sz, xc, bc, dt, alog_p[i], dskip_e[i], ssm_norm_w[i][None],
                  bsz=bsz, seq=seq, d_inner=d_inner)
        xt = _merge(xt, att, yn, gates, b_gates[i][None], wa_b[i], ws_b[i], wo_b[i])
        xt = _ffn(xt, ffn2_norm_w[i][None], f2g[i], f2u[i], f2d[i])
    return xt.reshape(bsz, seq, d_model)
```

```python
import functools

import jax
import jax.numpy as jnp
from jax import lax
from jax.experimental import pallas as pl
from jax.experimental.pallas import tpu as pltpu

F32 = jnp.float32
BF16 = jnp.bfloat16

NORM_EPS = 1e-6
ROPE_THETA = 10000.0
HEAD_DIM = 64
ATT_DILATIONS = (1, 4, 16)
ATT_BAND = 128
DEINTERLEAVE_STRIDE = 4
ATT_BLOCK_BATCH = 8
HEADS_PER_GROUP = 8
ATT_GROUP_COLS = HEADS_PER_GROUP * HEAD_DIM
SSM_HEAD_DIM = 64
N_SSM_GROUPS = 4
D_STATE = 128
CONV_WIDTH = 4
SSD_CHUNK = 128
SSD_CHUNKS_PER_STEP = 4
LANES = 128
CONV_CARRY_ROWS = 8
PROJ_CONV_ROWS = 256
PROJ_TN = 1024
VMEM_LIMIT = 56 * 1024 * 1024


def _cparams(n_axes):
    return pltpu.CompilerParams(dimension_semantics=("arbitrary",) * n_axes,
                                vmem_limit_bytes=VMEM_LIMIT)


def _rms_norm(x, w):
    return x * lax.rsqrt(jnp.mean(x * x, axis=-1, keepdims=True) + NORM_EPS) * w


def _sigmoid(x):
    return 0.5 * jnp.tanh(0.5 * x) + 0.5


def _silu(x):
    h = 0.5 * x
    return h * jnp.tanh(h) + h


def _softplus(x):
    return jnp.maximum(x, 0.0) + jnp.log1p(jnp.exp(-jnp.abs(x)))


def _resident(shape):
    return pl.BlockSpec(shape, lambda *_: (0,) * len(shape), pipeline_mode=pl.Buffered(1))


def _ffn_kernel(x_ref, nw_ref, wg_ref, wu_ref, wd_ref, *rest):
    x = x_ref[...]
    h = _rms_norm(x, nw_ref[...]).astype(BF16)
    g = jnp.dot(h, wg_ref[...], preferred_element_type=F32)
    u = jnp.dot(h, wu_ref[...], preferred_element_type=F32)
    a = (_silu(g) * u).astype(BF16)
    y = jnp.dot(a, wd_ref[...], preferred_element_type=F32)
    out = x + 0.5 * y
    if len(rest) == 1:
        rest[0][...] = out
    else:
        nw_next_ref, o_ref, h_ref = rest
        o_ref[...] = out
        h_ref[...] = _rms_norm(out, nw_next_ref[...]).astype(BF16)


def _ffn(x, nw, wg, wu, wd, nw_next=None, tm=512):
    t, d = x.shape
    f = wg.shape[1]
    row = pl.BlockSpec((tm, d), lambda i: (i, 0))
    in_specs = [row, _resident((1, d)), _resident((d, f)), _resident((d, f)), _resident((f, d))]
    args = [x, nw, wg, wu, wd]
    out_specs, out_shape = row, jax.ShapeDtypeStruct((t, d), F32)
    if nw_next is not None:
        in_specs.append(_resident((1, d)))
        args.append(nw_next)
        out_specs, out_shape = [row, row], [out_shape, jax.ShapeDtypeStruct((t, d), BF16)]
    return pl.pallas_call(
        _ffn_kernel,
        grid=(t // tm,),
        in_specs=in_specs,
        out_specs=out_specs,
        out_shape=out_shape,
        compiler_params=_cparams(1),
        name="ffn",
    )(*args)


def _proj_kernel(h_ref, w_ref, wdt_ref, cw_ref, cb_ref, dtb_ref, x_ref, sz_ref, g_ref, bc_ref, dt_ref,
                 conv_scr, conv_scr2, hist_scr, *, tm, tn, tiles_per_seq):
    i = pl.program_id(0)
    j = pl.program_id(1)
    hist = CONV_CARRY_ROWS
    sub = PROJ_CONV_ROWS
    n_sub = tm // sub
    bufs = (conv_scr, conv_scr2)

    def matmul(s, half):
        return jnp.dot(h_ref[pl.ds(s * sub, sub), :], w_ref[:, half * tn:(half + 1) * tn],
                       preferred_element_type=F32)

    def conv_start(slot):
        @pl.when(i % tiles_per_seq == 0)
        def _():
            conv_scr[pl.ds(0, hist), :] = jnp.zeros((hist, tn), F32)

        @pl.when(i % tiles_per_seq != 0)
        def _():
            conv_scr[pl.ds(0, hist), :] = hist_scr[slot]

    def conv_rows(slot, s, out_ref):
        buf = bufs[s % 2]
        buf[pl.ds(hist, sub), :] = matmul(s, 0)
        rows = buf[...]
        acc = cb_ref[...] + cw_ref[CONV_WIDTH - 1:CONV_WIDTH, :] * rows[hist:]
        for k in range(1, CONV_WIDTH):
            acc = acc + (cw_ref[CONV_WIDTH - 1 - k:CONV_WIDTH - k, :]
                         * pltpu.roll(rows, k, 0)[hist:])
        out_ref[pl.ds(s * sub, sub), :] = _silu(acc).astype(out_ref.dtype)
        tail = buf[pl.ds(sub, hist), :]
        if s + 1 < n_sub:
            bufs[(s + 1) % 2][pl.ds(0, hist), :] = tail
        else:
            hist_scr[slot] = tail

    for jj in range(2):
        @pl.when(j == jj)
        def _(jj=jj):
            conv_start(jj)
            for s in range(n_sub):
                conv_rows(jj, s, x_ref)
                sz_ref[pl.ds(s * sub, sub), :] = _silu(matmul(s, 1))

    @pl.when(j == 2)
    def _():
        conv_start(2)
        for s in range(n_sub):
            conv_rows(2, s, bc_ref)
            g_ref[pl.ds(s * sub, sub), :] = matmul(s, 1)

    @pl.when(j == 3)
    def _():
        for s in range(n_sub):
            g_ref[pl.ds(s * sub, sub), :] = matmul(s, 0)
        dt_raw = jnp.dot(h_ref[...], wdt_ref[...], preferred_element_type=F32)
        dt_ref[...] = _softplus(dt_raw + dtb_ref[...])


def _proj(h, w_pairs, w_dt, conv_w, conv_b, dt_bias, *, seq, d_inner, d_model, tm=1024, tn=PROJ_TN):
    t, d = h.shape
    bc_w = 2 * N_SSM_GROUPS * D_STATE
    assert bc_w == tn and d_inner == 2 * tn and 2 * d_model == 2 * tn and w_pairs.shape[1] == 8 * tn
    f32_out = jax.ShapeDtypeStruct((t, 2 * tn), F32)
    return pl.pallas_call(
        functools.partial(_proj_kernel, tm=tm, tn=tn, tiles_per_seq=seq // tm),
        grid=(t // tm, 4),
        in_specs=[pl.BlockSpec((tm, d), lambda i, j: (i, 0)),
                  pl.BlockSpec((d, 2 * tn), lambda i, j: (0, j)),
                  pl.BlockSpec((d, LANES), lambda i, j: (0, 0)),
                  pl.BlockSpec((CONV_WIDTH, tn), lambda i, j: (0, jnp.minimum(j, 2))),
                  pl.BlockSpec((1, tn), lambda i, j: (0, jnp.minimum(j, 2))),
                  pl.BlockSpec((1, LANES), lambda i, j: (0, 0))],
        out_specs=[pl.BlockSpec((tm, tn), lambda i, j: (i, jnp.minimum(j, 1))),
                   pl.BlockSpec((tm, tn), lambda i, j: (i, jnp.minimum(j, 1))),
                   pl.BlockSpec((tm, tn), lambda i, j: (i, jnp.maximum(j, 2) - 2)),
                   pl.BlockSpec((tm, tn), lambda i, j: (i, 0)),
                   pl.BlockSpec((tm, LANES), lambda i, j: (i, 0))],
        out_shape=[f32_out, f32_out, f32_out,
                   jax.ShapeDtypeStruct((t, bc_w), BF16),
                   jax.ShapeDtypeStruct((t, LANES), F32)],
        scratch_shapes=[pltpu.VMEM((PROJ_CONV_ROWS + CONV_CARRY_ROWS, tn), F32),
                        pltpu.VMEM((PROJ_CONV_ROWS + CONV_CARRY_ROWS, tn), F32),
                        pltpu.VMEM((3, CONV_CARRY_ROWS, tn), F32)],
        compiler_params=_cparams(2),
        name="proj_zxg",
    )(h, w_pairs, w_dt, conv_w, conv_b, dt_bias)


def _qkv_kernel(h_ref, w_ref, hnw_ref, cos_ref, sin_ref, bd_ref,
                q_ref, k_ref, v_ref, res_scr, tmp_scr, *, dil, tm):
    n = tm // dil
    gc = ATT_GROUP_COLS
    n_chunks = gc // LANES
    h = h_ref[...]
    lane = lax.broadcasted_iota(jnp.int32, (tm, LANES), 1)
    first_half = (lane % HEAD_DIM) < (HEAD_DIM // 2)

    def store_classes(part, out_ref):
        for c in range(n_chunks):
            cols = slice(c * LANES, (c + 1) * LANES)
            src = res_scr.at[part, c]
            if dil == 1:
                out_ref[0, :, cols] = src[...].astype(BF16)
            elif dil <= DEINTERLEAVE_STRIDE:
                for r in range(dil):
                    out_ref[r, :, cols] = src[pl.ds(r, n, stride=dil), :].astype(BF16)
            else:
                s1 = DEINTERLEAVE_STRIDE
                s2 = dil // s1
                n1 = tm // s1
                tmp = tmp_scr.at[part]
                for r1 in range(s1):
                    tmp[pl.ds(r1 * n1, n1), :] = src[pl.ds(r1, n1, stride=s1), :]
                for r1 in range(s1):
                    for r2 in range(s2):
                        out_ref[r1 + s1 * r2, :, cols] = (
                            tmp[pl.ds(r1 * n1 + r2, n, stride=s2), :].astype(BF16))

    for part, out_ref in enumerate((q_ref, k_ref, v_ref)):
        res = jnp.dot(h, w_ref[:, part * gc:(part + 1) * gc], preferred_element_type=F32)
        for c in range(n_chunks):
            xr = res[:, c * LANES:(c + 1) * LANES]
            if part < 2:
                x2 = xr * xr
                hi = x2.astype(BF16)
                lo = (x2 - hi.astype(F32)).astype(BF16)
                ms = (jnp.dot(hi, bd_ref[...], preferred_element_type=F32)
                      + jnp.dot(lo, bd_ref[...], preferred_element_type=F32))
                y = xr * lax.rsqrt(ms + NORM_EPS) * hnw_ref[part]
                partner = jnp.where(first_half,
                                    pltpu.roll(y, LANES - HEAD_DIM // 2, 1),
                                    pltpu.roll(y, HEAD_DIM // 2, 1))
                xr = y * cos_ref[...] + partner * sin_ref[...]
            res_scr[part, c] = xr
        store_classes(part, out_ref)


def _qkv(h, w, hnw, cos, sin, bd, *, bsz, seq, dil, tm=1024):
    t, d = h.shape
    tiles_per_seq = seq // tm
    n = tm // dil
    gc = ATT_GROUP_COLS
    assert dil <= DEINTERLEAVE_STRIDE or dil % DEINTERLEAVE_STRIDE == 0
    out_sds = jax.ShapeDtypeStruct((bsz, dil, seq // dil, gc), BF16)
    out_spec = pl.BlockSpec((None, dil, n, gc), lambda i: (i // tiles_per_seq, 0, i % tiles_per_seq, 0))
    return pl.pallas_call(
        functools.partial(_qkv_kernel, dil=dil, tm=tm),
        grid=(t // tm,),
        in_specs=[pl.BlockSpec((tm, d), lambda i: (i, 0)),
                  _resident((d, 3 * gc)),
                  _resident((2, 1, LANES)),
                  pl.BlockSpec((tm, LANES), lambda i: (i % tiles_per_seq, 0)),
                  pl.BlockSpec((tm, LANES), lambda i: (i % tiles_per_seq, 0)),
                  _resident((LANES, LANES))],
        out_specs=[out_spec, out_spec, out_spec],
        out_shape=[out_sds, out_sds, out_sds],
        scratch_shapes=[pltpu.VMEM((3, gc // LANES, tm, LANES), F32), pltpu.VMEM((3, tm, LANES), F32)],
        compiler_params=_cparams(1),
        name=f"qkv_d{dil}",
    )(h, w, hnw, cos, sin, bd)


def _attn_kernel(q0, k0, v0, q1, k1, v1, q2, k2, v2, o_ref, acc_o, acc_m, acc_l, *, seq):
    band = ATT_BAND
    nbatch = ATT_BLOCK_BATCH
    lane = lax.broadcasted_iota(jnp.int32, (band, LANES), 1)
    head0 = lane < HEAD_DIM
    iq = lax.broadcasted_iota(jnp.int32, (band, band), 0)
    ik = lax.broadcasted_iota(jnp.int32, (band, band), 1)
    cur_mask = ik <= iq
    prev_mask = ik >= iq
    nt_dims = (((2,), (2,)), ((0,), (0,)))
    nn_dims = (((2,), (1,)), ((0,), (0,)))

    cur_mask2 = jnp.concatenate([cur_mask, cur_mask], axis=0)
    prev_mask2 = jnp.concatenate([prev_mask, prev_mask], axis=0)

    def blocks(qb, kc, vc, kp=None, vp=None, first_has_no_prev=False):
        zero = jnp.zeros_like(qb)
        ones = jnp.ones(vc.shape, BF16)
        qs = jnp.concatenate([jnp.where(head0, qb, zero), jnp.where(head0, zero, qb)], axis=1)
        sc = lax.dot_general(qs, kc, nt_dims, preferred_element_type=F32)
        sc = jnp.where(cur_mask2, sc, -jnp.inf)
        m = jnp.max(sc, axis=-1, keepdims=True)
        if kp is not None:
            sp = lax.dot_general(qs, kp, nt_dims, preferred_element_type=F32)
            sp = jnp.where(prev_mask2, sp, -jnp.inf)
            if first_has_no_prev:
                bidx = lax.broadcasted_iota(jnp.int32, sp.shape, 0)
                sp = jnp.where(bidx > 0, sp, -jnp.inf)
            m = jnp.maximum(m, jnp.max(sp, axis=-1, keepdims=True))
        acc = lax.dot_general(jnp.exp(sc - m).astype(BF16), jnp.concatenate([vc, ones], axis=-1),
                              nn_dims, preferred_element_type=F32)
        if kp is not None:
            acc = acc + lax.dot_general(jnp.exp(sp - m).astype(BF16),
                                        jnp.concatenate([vp, ones], axis=-1),
                                        nn_dims, preferred_element_type=F32)
        top, bot = acc[:, :band], acc[:, band:]
        return (jnp.where(head0, top[..., :LANES], bot[..., :LANES]),
                jnp.where(head0, m[:, :band], m[:, band:]),
                jnp.where(head0, top[..., LANES:], bot[..., LANES:]))

    def store(g, rows, res, b=None):
        for acc, val in zip((acc_o, acc_m, acc_l), res):
            acc[g, rows, :] = val.reshape(-1, LANES) if b is None else val[b]

    span = nbatch * band
    for n0 in range(0, seq // band, nbatch):
        def view(ref, start):
            return ref[0, pl.ds(start, span), :].reshape(nbatch, band, LANES)
        if n0 == 0:
            def shifted(ref):
                return jnp.concatenate([ref[0, pl.ds(0, band), :], ref[0, pl.ds(0, span - band), :]],
                                       axis=0).reshape(nbatch, band, LANES)
            kp, vp = shifted(k0), shifted(v0)
        else:
            kp, vp = view(k0, (n0 - 1) * band), view(v0, (n0 - 1) * band)
        res = blocks(view(q0, n0 * band), view(k0, n0 * band), view(v0, n0 * band), kp, vp,
                     first_has_no_prev=(n0 == 0))
        store(0, pl.ds(n0 * band, span), res)

    dil = ATT_DILATIONS[1]
    assert seq // dil // band == 4

    def class_blocks(ref, n, count):
        return ref[:, pl.ds(n * band, count * band), :].reshape(dil * count, band, LANES)

    def store_classes(res, n, count):
        for r in range(dil):
            for t in range(count):
                store(1, pl.ds(r + dil * band * (n + t), band, stride=dil), res, b=count * r + t)

    store_classes(blocks(class_blocks(q1, 0, 1), class_blocks(k1, 0, 1), class_blocks(v1, 0, 1)), 0, 1)
    for n, count in ((1, 2), (3, 1)):
        store_classes(blocks(class_blocks(q1, n, count), class_blocks(k1, n, count), class_blocks(v1, n, count),
                             class_blocks(k1, n - 1, count), class_blocks(v1, n - 1, count)), n, count)

    dil = ATT_DILATIONS[2]
    assert seq // dil == band and dil % nbatch == 0 and (seq // band) % nbatch == 0
    for r0 in range(0, dil, nbatch):
        cls = pl.ds(r0, nbatch)
        res = blocks(q2[cls], k2[cls], v2[cls])
        for b in range(nbatch):
            store(2, pl.ds(r0 + b, band, stride=dil), res, b=b)

    def merge_rows(i, carry):
        rows = pl.ds(pl.multiple_of(i * band, band), band)
        m0, m1, m2 = acc_m[0, rows, :], acc_m[1, rows, :], acc_m[2, rows, :]
        m = jnp.maximum(jnp.maximum(m0, m1), m2)
        w0, w1, w2 = jnp.exp(m0 - m), jnp.exp(m1 - m), jnp.exp(m2 - m)
        num = w0 * acc_o[0, rows, :] + w1 * acc_o[1, rows, :] + w2 * acc_o[2, rows, :]
        den = w0 * acc_l[0, rows, :] + w1 * acc_l[1, rows, :] + w2 * acc_l[2, rows, :]
        o_ref[rows, :] = (num / den).astype(o_ref.dtype)
        return carry

    lax.fori_loop(0, seq // band, merge_rows, 0)


def _attention(qkv, *, bsz, seq):
    n_pairs = ATT_GROUP_COLS // LANES
    in_specs = []
    for dil in ATT_DILATIONS:
        spec = pl.BlockSpec((None, dil, seq // dil, LANES), lambda b, p: (b, 0, 0, p))
        in_specs += [spec, spec, spec]
    return pl.pallas_call(
        functools.partial(_attn_kernel, seq=seq),
        grid=(bsz, n_pairs),
        in_specs=in_specs,
        out_specs=pl.BlockSpec((None, seq, LANES), lambda b, p: (b, 0, p)),
        out_shape=jax.ShapeDtypeStruct((bsz, seq, ATT_GROUP_COLS), BF16),
        scratch_shapes=[pltpu.VMEM((len(ATT_DILATIONS), seq, LANES), F32)] * 3,
        compiler_params=_cparams(2),
        name="dilated_attn",
    )(*qkv)


def _ssd_kernel(sz_ref, x_ref, bc_ref, dt_ref, alog_ref, dskip_ref, nw_ref, o_ref,
                st_scr, y_scr, cum_scr, cumt_scr, cumdt_scr, swt_scr):
    @pl.when(pl.program_id(1) == 0)
    def _():
        st_scr[...] = jnp.zeros_like(st_scr)

    for ci in range(SSD_CHUNKS_PER_STEP):
        _ssd_chunk(pl.ds(ci * SSD_CHUNK, SSD_CHUNK), sz_ref, x_ref, bc_ref, dt_ref, alog_ref, dskip_ref,
                   nw_ref, o_ref, st_scr,
                   *(r.at[ci] for r in (y_scr, cum_scr, cumt_scr, cumdt_scr, swt_scr)))


def _ssd_chunk(rows, sz_ref, x_ref, bc_ref, dt_ref, alog_ref, dskip_ref, nw_ref, o_ref,
               st_scr, y_scr, cum_scr, cumt_scr, cumdt_scr, swt_scr):
    L = SSD_CHUNK
    n_state = D_STATE
    d_inner = x_ref.shape[-1]
    n_heads = d_inner // SSM_HEAD_DIM
    heads_per_group = n_heads // N_SSM_GROUPS
    group_w = d_inner // N_SSM_GROUPS

    dt = dt_ref[rows, :]
    a = -jnp.exp(alog_ref[...])
    ti = lax.broadcasted_iota(jnp.int32, (L, L), 0)
    tj = lax.broadcasted_iota(jnp.int32, (L, L), 1)
    causal = ti >= tj
    cum = jnp.dot(causal.astype(F32), dt * a, preferred_element_type=F32,
                  precision=lax.Precision.HIGHEST)
    cum_t = cum.T
    dt_t = dt.T
    cum_scr[...] = cum
    cumt_scr[...] = cum_t
    cumdt_scr[...] = cum_t - jnp.log(dt_t)
    swt_scr[...] = dt_t * jnp.exp(cum_t[:, L - 1:L] - cum_t)

    lane = lax.broadcasted_iota(jnp.int32, (L, LANES), 1)
    head0 = lane < SSM_HEAD_DIM

    for g in range(N_SSM_GROUPS):
        b_g = bc_ref[rows, g * n_state:(g + 1) * n_state]
        c_g = bc_ref[rows, (N_SSM_GROUPS + g) * n_state:(N_SSM_GROUPS + g + 1) * n_state]
        bt_g = b_g.astype(F32).T
        c_gf = c_g.astype(F32)
        cb = jnp.dot(c_g, bt_g.astype(BF16), preferred_element_type=F32)
        for pair in range(heads_per_group // 2):
            col = g * group_w + pair * LANES
            cols = slice(col, col + LANES)
            x_pair = x_ref[rows, cols]
            x_pair_b = x_pair.astype(BF16)
            st_pair = st_scr[:, cols]
            rhs = jnp.concatenate([x_pair_b, st_pair.astype(BF16)], axis=0)
            ys, ds, es = [], [], []
            for h in (g * heads_per_group + 2 * pair, g * heads_per_group + 2 * pair + 1):
                colb = jnp.broadcast_to(cum_scr[:, h:h + 1], (L, L))
                dec_dt = jnp.exp(jnp.where(causal, colb - cumdt_scr[h:h + 1, :], -jnp.inf))
                m_h = (cb * dec_dt).astype(BF16)
                c_h = (c_gf * jnp.exp(colb)).astype(BF16)
                ys.append(jnp.dot(jnp.concatenate([m_h, c_h], axis=1), rhs,
                                  preferred_element_type=F32))
                bt_h = (bt_g * swt_scr[h:h + 1, :]).astype(BF16)
                ds.append(jnp.dot(bt_h, x_pair_b, preferred_element_type=F32))
                es.append(jnp.exp(cumt_scr[h:h + 1, L - 1:L]))
            st_scr[:, cols] = (st_pair * jnp.where(head0, es[0], es[1])
                               + jnp.where(head0, ds[0], ds[1]))
            y_scr[:, cols] = jnp.where(head0, ys[0], ys[1]) + dskip_ref[:, cols] * x_pair

    for g in range(N_SSM_GROUPS):
        cols = slice(g * group_w, (g + 1) * group_w)
        y = y_scr[:, cols] * sz_ref[rows, cols]
        o_ref[rows, cols] = _rms_norm(y, nw_ref[:, cols]).astype(o_ref.dtype)


def _ssd(sz, x, bc, dt, alog, dskip, nw, *, bsz, seq, d_inner):
    L = SSD_CHUNK * SSD_CHUNKS_PER_STEP
    nck = SSD_CHUNKS_PER_STEP
    bc_w = bc.shape[-1]

    def small(shape):
        return pl.BlockSpec(shape, lambda b, c: (0,) * len(shape))

    out = pl.pallas_call(
        _ssd_kernel,
        grid=(bsz, seq // L),
        in_specs=[pl.BlockSpec((None, L, d_inner), lambda b, c: (b, c, 0)),
                  pl.BlockSpec((None, L, d_inner), lambda b, c: (b, c, 0)),
                  pl.BlockSpec((None, L, bc_w), lambda b, c: (b, c, 0)),
                  pl.BlockSpec((None, L, LANES), lambda b, c: (b, c, 0)),
                  small((1, LANES)), small((1, d_inner)), small((1, d_inner))],
        out_specs=pl.BlockSpec((None, L, d_inner), lambda b, c: (b, c, 0)),
        out_shape=jax.ShapeDtypeStruct((bsz, seq, d_inner), BF16),
        scratch_shapes=[pltpu.VMEM((D_STATE, d_inner), F32),
                        pltpu.VMEM((nck, SSD_CHUNK, d_inner), F32),
                        pltpu.VMEM((nck, SSD_CHUNK, LANES), F32),
                        pltpu.VMEM((nck, LANES, SSD_CHUNK), F32),
                        pltpu.VMEM((nck, LANES, SSD_CHUNK), F32),
                        pltpu.VMEM((nck, LANES, SSD_CHUNK), F32)],
        compiler_params=_cparams(2),
        name="ssd",
    )(sz.reshape(bsz, seq, d_inner), x.reshape(bsz, seq, d_inner), bc.reshape(bsz, seq, bc_w),
      dt.reshape(bsz, seq, LANES), alog, dskip, nw)
    return out.reshape(bsz * seq, d_inner)


def _merge_kernel(x_ref, att_ref, yn_ref, gp_ref, bg_ref, wa_ref, ws_ref, wo_ref, o_ref):
    d = x_ref.shape[-1]
    ya = jnp.dot(att_ref[...], wa_ref[...], preferred_element_type=F32)
    ys = jnp.dot(yn_ref[...], ws_ref[...], preferred_element_type=F32)
    gates = _sigmoid(gp_ref[...] + bg_ref[...])
    mixed = (gates[:, :d] * ya + gates[:, d:] * ys).astype(BF16)
    o_ref[...] = x_ref[...] + jnp.dot(mixed, wo_ref[...], preferred_element_type=F32)


def _merge(x, att, yn, gates, bg, wa, ws, wo, tm=512):
    t, d = x.shape
    return pl.pallas_call(
        _merge_kernel,
        grid=(t // tm,),
        in_specs=[pl.BlockSpec((tm, d), lambda i: (i, 0)),
                  pl.BlockSpec((tm, att.shape[1]), lambda i: (i, 0)),
                  pl.BlockSpec((tm, yn.shape[1]), lambda i: (i, 0)),
                  pl.BlockSpec((tm, 2 * d), lambda i: (i, 0)),
                  _resident((1, 2 * d)), _resident(wa.shape), _resident(ws.shape), _resident(wo.shape)],
        out_specs=pl.BlockSpec((tm, d), lambda i: (i, 0)),
        out_shape=jax.ShapeDtypeStruct((t, d), F32),
        compiler_params=_cparams(1),
        name="merge",
    )(x, att, yn, gates, bg, wa, ws, wo)


def _rope_tables(seq):
    pos = jnp.arange(seq, dtype=F32)
    inv_freq = 1.0 / (ROPE_THETA ** (jnp.arange(0, HEAD_DIM, 2, dtype=F32) / HEAD_DIM))
    ang = pos[:, None] * inv_freq[None, :]
    cos, sin = jnp.cos(ang), jnp.sin(ang)
    reps = LANES // HEAD_DIM
    return (jnp.tile(jnp.concatenate([cos, cos], axis=1), (1, reps)),
            jnp.tile(jnp.concatenate([-sin, sin], axis=1), (1, reps)))


def kernel(x, ffn1_norm_w, ffn1_w_gate, ffn1_w_up, ffn1_w_down, mix_norm_w, w_in, b_gates,
           q_norm_w, k_norm_w, conv_w, conv_b, dt_bias, a_log, d_skip, ssm_norm_w,
           w_att_proj, w_ssm_proj, w_out, ffn2_norm_w, ffn2_w_gate, ffn2_w_up, ffn2_w_down):
    bsz, seq, d_model = x.shape
    depth = w_in.shape[0]
    d_inner = ssm_norm_w.shape[1]
    n_ssm_heads = dt_bias.shape[1]
    qkv_cols = len(ATT_DILATIONS) * ATT_GROUP_COLS
    bc_w = 2 * N_SSM_GROUPS * D_STATE
    assert seq % (ATT_DILATIONS[-1] * ATT_BAND) == 0 and seq % 1024 == 0
    assert seq % (SSD_CHUNK * SSD_CHUNKS_PER_STEP) == 0
    assert w_in.shape[2] == 3 * qkv_cols + d_inner + (d_inner + bc_w) + n_ssm_heads + 2 * d_model
    assert n_ssm_heads <= LANES

    o_z = 3 * qkv_cols
    o_x = o_z + d_inner
    o_bc = o_x + d_inner
    o_dt = o_bc + bc_w
    o_g = o_dt + n_ssm_heads
    gc = ATT_GROUP_COLS
    w_groups = [jnp.concatenate([w_in[:, :, s * qkv_cols + g * gc: s * qkv_cols + (g + 1) * gc]
                                 for s in range(3)], axis=2).astype(BF16)
                for g in range(len(ATT_DILATIONS))]
    tn = PROJ_TN
    assert d_inner == 2 * tn and d_model == tn and bc_w == tn
    z_w, x_w, bc_wt, g_w = w_in[:, :, o_z:o_x], w_in[:, :, o_x:o_bc], w_in[:, :, o_bc:o_dt], w_in[:, :, o_g:]
    w_pairs = jnp.concatenate([x_w[:, :, :tn], z_w[:, :, :tn], x_w[:, :, tn:], z_w[:, :, tn:],
                               bc_wt, g_w[:, :, :tn], g_w[:, :, tn:], jnp.zeros_like(bc_wt)],
                              axis=2).astype(BF16)
    pad_h = LANES - n_ssm_heads
    w_dt = jnp.pad(w_in[:, :, o_dt:o_g], ((0, 0), (0, 0), (0, pad_h))).astype(BF16)
    f1g, f1u, f1d = ffn1_w_gate.astype(BF16), ffn1_w_up.astype(BF16), ffn1_w_down.astype(BF16)
    f2g, f2u, f2d = ffn2_w_gate.astype(BF16), ffn2_w_up.astype(BF16), ffn2_w_down.astype(BF16)
    wa_b, ws_b, wo_b = w_att_proj.astype(BF16), w_ssm_proj.astype(BF16), w_out.astype(BF16)

    reps = LANES // HEAD_DIM
    head_nw = jnp.stack([jnp.tile(q_norm_w * (HEAD_DIM ** -0.5), (1, reps)),
                         jnp.tile(k_norm_w, (1, reps))], axis=1)[:, :, None, :]
    cos_t, sin_t = _rope_tables(seq)
    hid = jnp.arange(LANES) // HEAD_DIM
    block_diag = jnp.where(hid[:, None] == hid[None, :], 1.0 / HEAD_DIM, 0.0).astype(BF16)
    dtb_p = jnp.pad(dt_bias, ((0, 0), (0, pad_h)))[:, None, :]
    alog_p = jnp.pad(a_log, ((0, 0), (0, pad_h)))[:, None, :]
    dskip_e = jnp.repeat(d_skip, SSM_HEAD_DIM, axis=1)[:, None, :]

    xt = x.reshape(bsz * seq, d_model)
    for i in range(depth):
        xt, h = _ffn(xt, ffn1_norm_w[i][None], f1g[i], f1u[i], f1d[i], mix_norm_w[i][None])
        qkv = []
        for g, dil in enumerate(ATT_DILATIONS):
            qkv += _qkv(h, w_groups[g][i], head_nw[i], cos_t, sin_t, block_diag,
                        bsz=bsz, seq=seq, dil=dil)
        xc, sz, gates, bc, dt = _proj(h, w_pairs[i], w_dt[i], conv_w[i], conv_b[i][None], dtb_p[i],
                                      seq=seq, d_inner=d_inner, d_model=d_model)
        att = _attention(qkv, bsz=bsz, seq=seq).reshape(bsz * seq, ATT_GROUP_COLS)
        yn = _ssd(sz, xc, bc, dt, alog_p[i], dskip_e[i], ssm_norm_w[i][None],
                  bsz=bsz, seq=seq, d_inner=d_inner)
        xt = _merge(xt, att, yn, gates, b_gates[i][None], wa_b[i], ws_b[i], wo_b[i])
        xt = _ffn(xt, ffn2_norm_w[i][None], f2g[i], f2u[i], f2d[i])
    return xt.reshape(bsz, seq, d_model)
```

```python
import functools

import jax
import jax.numpy as jnp
from jax import lax
from jax.experimental import pallas as pl
from jax.experimental.pallas import tpu as pltpu

F32 = jnp.float32
BF16 = jnp.bfloat16

NORM_EPS = 1e-6
ROPE_THETA = 10000.0
HEAD_DIM = 64
ATT_DILATIONS = (1, 4, 16)
ATT_BAND = 128
DEINTERLEAVE_STRIDE = 4
ATT_BLOCK_BATCH = 8
HEADS_PER_GROUP = 8
ATT_GROUP_COLS = HEADS_PER_GROUP * HEAD_DIM
SSM_HEAD_DIM = 64
N_SSM_GROUPS = 4
D_STATE = 128
CONV_WIDTH = 4
SSD_CHUNK = 128
SSD_CHUNKS_PER_STEP = 4
LANES = 128
CONV_CARRY_ROWS = 8
PROJ_CONV_ROWS = 256
PROJ_TN = 1024
VMEM_LIMIT = 56 * 1024 * 1024


def _cparams(n_axes):
    return pltpu.CompilerParams(dimension_semantics=("arbitrary",) * n_axes,
                                vmem_limit_bytes=VMEM_LIMIT)


def _rms_norm(x, w):
    return x * lax.rsqrt(jnp.mean(x * x, axis=-1, keepdims=True) + NORM_EPS) * w


def _sigmoid(x):
    return 0.5 * jnp.tanh(0.5 * x) + 0.5


def _silu(x):
    h = 0.5 * x
    return h * jnp.tanh(h) + h


def _softplus(x):
    return jnp.maximum(x, 0.0) + jnp.log1p(jnp.exp(-jnp.abs(x)))


def _resident(shape):
    return pl.BlockSpec(shape, lambda *_: (0,) * len(shape), pipeline_mode=pl.Buffered(1))


def _ffn_kernel(x_ref, nw_ref, wg_ref, wu_ref, wd_ref, *rest):
    x = x_ref[...]
    h = _rms_norm(x, nw_ref[...]).astype(BF16)
    g = jnp.dot(h, wg_ref[...], preferred_element_type=F32)
    u = jnp.dot(h, wu_ref[...], preferred_element_type=F32)
    a = (_silu(g) * u).astype(BF16)
    y = jnp.dot(a, wd_ref[...], preferred_element_type=F32)
    out = x + 0.5 * y
    if len(rest) == 1:
        rest[0][...] = out
    else:
        nw_next_ref, o_ref, h_ref = rest
        o_ref[...] = out
        h_ref[...] = _rms_norm(out, nw_next_ref[...]).astype(BF16)


def _ffn(x, nw, wg, wu, wd, nw_next=None, tm=512):
    t, d = x.shape
    f = wg.shape[1]
    row = pl.BlockSpec((tm, d), lambda i: (i, 0))
    in_specs = [row, _resident((1, d)), _resident((d, f)), _resident((d, f)), _resident((f, d))]
    args = [x, nw, wg, wu, wd]
    out_specs, out_shape = row, jax.ShapeDtypeStruct((t, d), F32)
    if nw_next is not None:
        in_specs.append(_resident((1, d)))
        args.append(nw_next)
        out_specs, out_shape = [row, row], [out_shape, jax.ShapeDtypeStruct((t, d), BF16)]
    return pl.pallas_call(
        _ffn_kernel,
        grid=(t // tm,),
        in_specs=in_specs,
        out_specs=out_specs,
        out_shape=out_shape,
        compiler_params=_cparams(1),
        name="ffn",
    )(*args)


def _proj_kernel(h_ref, w_ref, wdt_ref, cw_ref, cb_ref, dtb_ref, x_ref, sz_ref, g_ref, bc_ref, dt_ref,
                 conv_scr, conv_scr2, hist_scr, *, tm, tn, tiles_per_seq):
    i = pl.program_id(0)
    j = pl.program_id(1)
    hist = CONV_CARRY_ROWS
    sub = PROJ_CONV_ROWS
    n_sub = tm // sub
    bufs = (conv_scr, conv_scr2)

    def matmul(s, half):
        return jnp.dot(h_ref[pl.ds(s * sub, sub), :], w_ref[:, half * tn:(half + 1) * tn],
                       preferred_element_type=F32)

    def conv_start(slot):
        @pl.when(i % tiles_per_seq == 0)
        def _():
            conv_scr[pl.ds(0, hist), :] = jnp.zeros((hist, tn), F32)

        @pl.when(i % tiles_per_seq != 0)
        def _():
            conv_scr[pl.ds(0, hist), :] = hist_scr[slot]

    def conv_rows(slot, s, out_ref):
        buf = bufs[s % 2]
        buf[pl.ds(hist, sub), :] = matmul(s, 0)
        rows = buf[...]
        acc = cb_ref[...] + cw_ref[CONV_WIDTH - 1:CONV_WIDTH, :] * rows[hist:]
        for k in range(1, CONV_WIDTH):
            acc = acc + (cw_ref[CONV_WIDTH - 1 - k:CONV_WIDTH - k, :]
                         * pltpu.roll(rows, k, 0)[hist:])
        out_ref[pl.ds(s * sub, sub), :] = _silu(acc).astype(out_ref.dtype)
        tail = buf[pl.ds(sub, hist), :]
        if s + 1 < n_sub:
            bufs[(s + 1) % 2][pl.ds(0, hist), :] = tail
        else:
            hist_scr[slot] = tail

    for jj in range(2):
        @pl.when(j == jj)
        def _(jj=jj):
            conv_start(jj)
            for s in range(n_sub):
                conv_rows(jj, s, x_ref)
                sz_ref[pl.ds(s * sub, sub), :] = _silu(matmul(s, 1))

    @pl.when(j == 2)
    def _():
        conv_start(2)
        for s in range(n_sub):
            conv_rows(2, s, bc_ref)
            g_ref[pl.ds(s * sub, sub), :] = matmul(s, 1)

    @pl.when(j == 3)
    def _():
        for s in range(n_sub):
            g_ref[pl.ds(s * sub, sub), :] = matmul(s, 0)
        dt_raw = jnp.dot(h_ref[...], wdt_ref[...], preferred_element_type=F32)
        dt_ref[...] = _softplus(dt_raw + dtb_ref[...])


def _proj(h, w_pairs, w_dt, conv_w, conv_b, dt_bias, *, seq, d_inner, d_model, tm=1024, tn=PROJ_TN):
    t, d = h.shape
    bc_w = 2 * N_SSM_GROUPS * D_STATE
    assert bc_w == tn and d_inner == 2 * tn and 2 * d_model == 2 * tn and w_pairs.shape[1] == 8 * tn
    f32_out = jax.ShapeDtypeStruct((t, 2 * tn), F32)
    return pl.pallas_call(
        functools.partial(_proj_kernel, tm=tm, tn=tn, tiles_per_seq=seq // tm),
        grid=(t // tm, 4),
        in_specs=[pl.BlockSpec((tm, d), lambda i, j: (i, 0)),
                  pl.BlockSpec((d, 2 * tn), lambda i, j: (0, j)),
                  pl.BlockSpec((d, LANES), lambda i, j: (0, 0)),
                  pl.BlockSpec((CONV_WIDTH, tn), lambda i, j: (0, jnp.minimum(j, 2))),
                  pl.BlockSpec((1, tn), lambda i, j: (0, jnp.minimum(j, 2))),
                  pl.BlockSpec((1, LANES), lambda i, j: (0, 0))],
        out_specs=[pl.BlockSpec((tm, tn), lambda i, j: (i, jnp.minimum(j, 1))),
                   pl.BlockSpec((tm, tn), lambda i, j: (i, jnp.minimum(j, 1))),
                   pl.BlockSpec((tm, tn), lambda i, j: (i, jnp.maximum(j, 2) - 2)),
                   pl.BlockSpec((tm, tn), lambda i, j: (i, 0)),
                   pl.BlockSpec((tm, LANES), lambda i, j: (i, 0))],
        out_shape=[f32_out, f32_out, f32_out,
                   jax.ShapeDtypeStruct((t, bc_w), BF16),
                   jax.ShapeDtypeStruct((t, LANES), F32)],
        scratch_shapes=[pltpu.VMEM((PROJ_CONV_ROWS + CONV_CARRY_ROWS, tn), F32),
                        pltpu.VMEM((PROJ_CONV_ROWS + CONV_CARRY_ROWS, tn), F32),
                        pltpu.VMEM((3, CONV_CARRY_ROWS, tn), F32)],
        compiler_params=_cparams(2),
        name="proj_zxg",
    )(h, w_pairs, w_dt, conv_w, conv_b, dt_bias)


def _qkv_kernel(h_ref, w_ref, hnw_ref, cos_ref, sin_ref, bd_ref,
                q_ref, k_ref, v_ref, res_scr, tmp_scr, *, dil, tm):
    n = tm // dil
    gc = ATT_GROUP_COLS
    n_chunks = gc // LANES
    h = h_ref[...]
    lane = lax.broadcasted_iota(jnp.int32, (tm, LANES), 1)
    first_half = (lane % HEAD_DIM) < (HEAD_DIM // 2)

    def store_classes(part, out_ref):
        for c in range(n_chunks):
            cols = slice(c * LANES, (c + 1) * LANES)
            src = res_scr.at[part, c]
            if dil == 1:
                out_ref[0, :, cols] = src[...].astype(BF16)
            elif dil <= DEINTERLEAVE_STRIDE:
                for r in range(dil):
                    out_ref[r, :, cols] = src[pl.ds(r, n, stride=dil), :].astype(BF16)
            else:
                s1 = DEINTERLEAVE_STRIDE
                s2 = dil // s1
                n1 = tm // s1
                tmp = tmp_scr.at[part]
                for r1 in range(s1):
                    tmp[pl.ds(r1 * n1, n1), :] = src[pl.ds(r1, n1, stride=s1), :]
                for r1 in range(s1):
                    for r2 in range(s2):
                        out_ref[r1 + s1 * r2, :, cols] = (
                            tmp[pl.ds(r1 * n1 + r2, n, stride=s2), :].astype(BF16))

    for part, out_ref in enumerate((q_ref, k_ref, v_ref)):
        res = jnp.dot(h, w_ref[:, part * gc:(part + 1) * gc], preferred_element_type=F32)
        for c in range(n_chunks):
            xr = res[:, c * LANES:(c + 1) * LANES]
            if part < 2:
                x2 = xr * xr
                hi = x2.astype(BF16)
                lo = (x2 - hi.astype(F32)).astype(BF16)
                ms = (jnp.dot(hi, bd_ref[...], preferred_element_type=F32)
                      + jnp.dot(lo, bd_ref[...], preferred_element_type=F32))
                y = xr * lax.rsqrt(ms + NORM_EPS) * hnw_ref[part]
                partner = jnp.where(first_half,
                                    pltpu.roll(y, LANES - HEAD_DIM // 2, 1),
                                    pltpu.roll(y, HEAD_DIM // 2, 1))
                xr = y * cos_ref[...] + partner * sin_ref[...]
            res_scr[part, c] = xr
        store_classes(part, out_ref)


def _qkv(h, w, hnw, cos, sin, bd, *, bsz, seq, dil, tm=2048):
    t, d = h.shape
    tiles_per_seq = seq // tm
    n = tm // dil
    gc = ATT_GROUP_COLS
    assert dil <= DEINTERLEAVE_STRIDE or dil % DEINTERLEAVE_STRIDE == 0
    out_sds = jax.ShapeDtypeStruct((bsz, dil, seq // dil, gc), BF16)
    out_spec = pl.BlockSpec((None, dil, n, gc), lambda i: (i // tiles_per_seq, 0, i % tiles_per_seq, 0))
    return pl.pallas_call(
        functools.partial(_qkv_kernel, dil=dil, tm=tm),
        grid=(t // tm,),
        in_specs=[pl.BlockSpec((tm, d), lambda i: (i, 0)),
                  _resident((d, 3 * gc)),
                  _resident((2, 1, LANES)),
                  pl.BlockSpec((tm, LANES), lambda i: (i % tiles_per_seq, 0)),
                  pl.BlockSpec((tm, LANES), lambda i: (i % tiles_per_seq, 0)),
                  _resident((LANES, LANES))],
        out_specs=[out_spec, out_spec, out_spec],
        out_shape=[out_sds, out_sds, out_sds],
        scratch_shapes=[pltpu.VMEM((3, gc // LANES, tm, LANES), F32), pltpu.VMEM((3, tm, LANES), F32)],
        compiler_params=_cparams(1),
        name=f"qkv_d{dil}",
    )(h, w, hnw, cos, sin, bd)


def _attn_kernel(q0, k0, v0, q1, k1, v1, q2, k2, v2, o_ref, acc_o, acc_m, acc_l, *, seq):
    band = ATT_BAND
    nbatch = ATT_BLOCK_BATCH
    lane = lax.broadcasted_iota(jnp.int32, (band, LANES), 1)
    head0 = lane < HEAD_DIM
    iq = lax.broadcasted_iota(jnp.int32, (band, band), 0)
    ik = lax.broadcasted_iota(jnp.int32, (band, band), 1)
    cur_mask = ik <= iq
    prev_mask = ik >= iq
    nt_dims = (((2,), (2,)), ((0,), (0,)))
    nn_dims = (((2,), (1,)), ((0,), (0,)))

    cur_mask2 = jnp.concatenate([cur_mask, cur_mask], axis=0)
    prev_mask2 = jnp.concatenate([prev_mask, prev_mask], axis=0)

    def blocks(qb, kc, vc, kp=None, vp=None, first_has_no_prev=False):
        zero = jnp.zeros_like(qb)
        ones = jnp.ones(vc.shape, BF16)
        qs = jnp.concatenate([jnp.where(head0, qb, zero), jnp.where(head0, zero, qb)], axis=1)
        sc = lax.dot_general(qs, kc, nt_dims, preferred_element_type=F32)
        sc = jnp.where(cur_mask2, sc, -jnp.inf)
        m = jnp.max(sc, axis=-1, keepdims=True)
        if kp is not None:
            sp = lax.dot_general(qs, kp, nt_dims, preferred_element_type=F32)
            sp = jnp.where(prev_mask2, sp, -jnp.inf)
            if first_has_no_prev:
                bidx = lax.broadcasted_iota(jnp.int32, sp.shape, 0)
                sp = jnp.where(bidx > 0, sp, -jnp.inf)
            m = jnp.maximum(m, jnp.max(sp, axis=-1, keepdims=True))
        acc = lax.dot_general(jnp.exp(sc - m).astype(BF16), jnp.concatenate([vc, ones], axis=-1),
                              nn_dims, preferred_element_type=F32)
        if kp is not None:
            acc = acc + lax.dot_general(jnp.exp(sp - m).astype(BF16),
                                        jnp.concatenate([vp, ones], axis=-1),
                                        nn_dims, preferred_element_type=F32)
        top, bot = acc[:, :band], acc[:, band:]
        return (jnp.where(head0, top[..., :LANES], bot[..., :LANES]),
                jnp.where(head0, m[:, :band], m[:, band:]),
                jnp.where(head0, top[..., LANES:], bot[..., LANES:]))

    def store(g, rows, res, b=None):
        for acc, val in zip((acc_o, acc_m, acc_l), res):
            acc[g, rows, :] = val.reshape(-1, LANES) if b is None else val[b]

    span = nbatch * band
    for n0 in range(0, seq // band, nbatch):
        def view(ref, start):
            return ref[0, pl.ds(start, span), :].reshape(nbatch, band, LANES)
        if n0 == 0:
            def shifted(ref):
                return jnp.concatenate([ref[0, pl.ds(0, band), :], ref[0, pl.ds(0, span - band), :]],
                                       axis=0).reshape(nbatch, band, LANES)
            kp, vp = shifted(k0), shifted(v0)
        else:
            kp, vp = view(k0, (n0 - 1) * band), view(v0, (n0 - 1) * band)
        res = blocks(view(q0, n0 * band), view(k0, n0 * band), view(v0, n0 * band), kp, vp,
                     first_has_no_prev=(n0 == 0))
        store(0, pl.ds(n0 * band, span), res)

    dil = ATT_DILATIONS[1]
    assert seq // dil // band == 4

    def class_blocks(ref, n, count):
        return ref[:, pl.ds(n * band, count * band), :].reshape(dil * count, band, LANES)

    def store_classes(res, n, count):
        for r in range(dil):
            for t in range(count):
                store(1, pl.ds(r + dil * band * (n + t), band, stride=dil), res, b=count * r + t)

    store_classes(blocks(class_blocks(q1, 0, 1), class_blocks(k1, 0, 1), class_blocks(v1, 0, 1)), 0, 1)
    for n, count in ((1, 2), (3, 1)):
        store_classes(blocks(class_blocks(q1, n, count), class_blocks(k1, n, count), class_blocks(v1, n, count),
                             class_blocks(k1, n - 1, count), class_blocks(v1, n - 1, count)), n, count)

    dil = ATT_DILATIONS[2]
    assert seq // dil == band and dil % nbatch == 0 and (seq // band) % nbatch == 0
    for r0 in range(0, dil, nbatch):
        cls = pl.ds(r0, nbatch)
        res = blocks(q2[cls], k2[cls], v2[cls])
        for b in range(nbatch):
            store(2, pl.ds(r0 + b, band, stride=dil), res, b=b)

    def merge_rows(i, carry):
        rows = pl.ds(pl.multiple_of(i * band, band), band)
        m0, m1, m2 = acc_m[0, rows, :], acc_m[1, rows, :], acc_m[2, rows, :]
        m = jnp.maximum(jnp.maximum(m0, m1), m2)
        w0, w1, w2 = jnp.exp(m0 - m), jnp.exp(m1 - m), jnp.exp(m2 - m)
        num = w0 * acc_o[0, rows, :] + w1 * acc_o[1, rows, :] + w2 * acc_o[2, rows, :]
        den = w0 * acc_l[0, rows, :] + w1 * acc_l[1, rows, :] + w2 * acc_l[2, rows, :]
        o_ref[rows, :] = (num / den).astype(o_ref.dtype)
        return carry

    lax.fori_loop(0, seq // band, merge_rows, 0)


def _attention(qkv, *, bsz, seq):
    n_pairs = ATT_GROUP_COLS // LANES
    in_specs = []
    for dil in ATT_DILATIONS:
        spec = pl.BlockSpec((None, dil, seq // dil, LANES), lambda b, p: (b, 0, 0, p))
        in_specs += [spec, spec, spec]
    return pl.pallas_call(
        functools.partial(_attn_kernel, seq=seq),
        grid=(bsz, n_pairs),
        in_specs=in_specs,
        out_specs=pl.BlockSpec((None, seq, LANES), lambda b, p: (b, 0, p)),
        out_shape=jax.ShapeDtypeStruct((bsz, seq, ATT_GROUP_COLS), BF16),
        scratch_shapes=[pltpu.VMEM((len(ATT_DILATIONS), seq, LANES), F32)] * 3,
        compiler_params=_cparams(2),
        name="dilated_attn",
    )(*qkv)


def _ssd_kernel(sz_ref, x_ref, bc_ref, dt_ref, alog_ref, dskip_ref, nw_ref, o_ref,
                st_scr, y_scr, cum_scr, cumt_scr, cumdt_scr, swt_scr):
    @pl.when(pl.program_id(1) == 0)
    def _():
        st_scr[...] = jnp.zeros_like(st_scr)

    for ci in range(SSD_CHUNKS_PER_STEP):
        _ssd_chunk(pl.ds(ci * SSD_CHUNK, SSD_CHUNK), sz_ref, x_ref, bc_ref, dt_ref, alog_ref, dskip_ref,
                   nw_ref, o_ref, st_scr,
                   *(r.at[ci] for r in (y_scr, cum_scr, cumt_scr, cumdt_scr, swt_scr)))


def _ssd_chunk(rows, sz_ref, x_ref, bc_ref, dt_ref, alog_ref, dskip_ref, nw_ref, o_ref,
               st_scr, y_scr, cum_scr, cumt_scr, cumdt_scr, swt_scr):
    L = SSD_CHUNK
    n_state = D_STATE
    d_inner = x_ref.shape[-1]
    n_heads = d_inner // SSM_HEAD_DIM
    heads_per_group = n_heads // N_SSM_GROUPS
    group_w = d_inner // N_SSM_GROUPS

    dt = dt_ref[rows, :]
    a = -jnp.exp(alog_ref[...])
    ti = lax.broadcasted_iota(jnp.int32, (L, L), 0)
    tj = lax.broadcasted_iota(jnp.int32, (L, L), 1)
    causal = ti >= tj
    cum = jnp.dot(causal.astype(F32), dt * a, preferred_element_type=F32,
                  precision=lax.Precision.HIGHEST)
    cum_t = cum.T
    dt_t = dt.T
    cum_scr[...] = cum
    cumt_scr[...] = cum_t
    cumdt_scr[...] = cum_t - jnp.log(dt_t)
    swt_scr[...] = dt_t * jnp.exp(cum_t[:, L - 1:L] - cum_t)

    lane = lax.broadcasted_iota(jnp.int32, (L, LANES), 1)
    head0 = lane < SSM_HEAD_DIM

    for g in range(N_SSM_GROUPS):
        b_g = bc_ref[rows, g * n_state:(g + 1) * n_state]
        c_g = bc_ref[rows, (N_SSM_GROUPS + g) * n_state:(N_SSM_GROUPS + g + 1) * n_state]
        bt_g = b_g.astype(F32).T
        c_gf = c_g.astype(F32)
        cb = jnp.dot(c_g, bt_g.astype(BF16), preferred_element_type=F32)
        for pair in range(heads_per_group // 2):
            col = g * group_w + pair * LANES
            cols = slice(col, col + LANES)
            x_pair = x_ref[rows, cols]
            x_pair_b = x_pair.astype(BF16)
            st_pair = st_scr[:, cols]
            rhs = jnp.concatenate([x_pair_b, st_pair.astype(BF16)], axis=0)
            ys, ds, es = [], [], []
            for h in (g * heads_per_group + 2 * pair, g * heads_per_group + 2 * pair + 1):
                colb = jnp.broadcast_to(cum_scr[:, h:h + 1], (L, L))
                dec_dt = jnp.exp(jnp.where(causal, colb - cumdt_scr[h:h + 1, :], -jnp.inf))
                m_h = (cb * dec_dt).astype(BF16)
                c_h = (c_gf * jnp.exp(colb)).astype(BF16)
                ys.append(jnp.dot(jnp.concatenate([m_h, c_h], axis=1), rhs,
                                  preferred_element_type=F32))
                bt_h = (bt_g * swt_scr[h:h + 1, :]).astype(BF16)
                ds.append(jnp.dot(bt_h, x_pair_b, preferred_element_type=F32))
                es.append(jnp.exp(cumt_scr[h:h + 1, L - 1:L]))
            st_scr[:, cols] = (st_pair * jnp.where(head0, es[0], es[1])
                               + jnp.where(head0, ds[0], ds[1]))
            y_scr[:, cols] = jnp.where(head0, ys[0], ys[1]) + dskip_ref[:, cols] * x_pair

    for g in range(N_SSM_GROUPS):
        cols = slice(g * group_w, (g + 1) * group_w)
        y = y_scr[:, cols] * sz_ref[rows, cols]
        o_ref[rows, cols] = _rms_norm(y, nw_ref[:, cols]).astype(o_ref.dtype)


def _ssd(sz, x, bc, dt, alog, dskip, nw, *, bsz, seq, d_inner):
    L = SSD_CHUNK * SSD_CHUNKS_PER_STEP
    nck = SSD_CHUNKS_PER_STEP
    bc_w = bc.shape[-1]

    def small(shape):
        return pl.BlockSpec(shape, lambda b, c: (0,) * len(shape))

    out = pl.pallas_call(
        _ssd_kernel,
        grid=(bsz, seq // L),
        in_specs=[pl.BlockSpec((None, L, d_inner), lambda b, c: (b, c, 0)),
                  pl.BlockSpec((None, L, d_inner), lambda b, c: (b, c, 0)),
                  pl.BlockSpec((None, L, bc_w), lambda b, c: (b, c, 0)),
                  pl.BlockSpec((None, L, LANES), lambda b, c: (b, c, 0)),
                  small((1, LANES)), small((1, d_inner)), small((1, d_inner))],
        out_specs=pl.BlockSpec((None, L, d_inner), lambda b, c: (b, c, 0)),
        out_shape=jax.ShapeDtypeStruct((bsz, seq, d_inner), BF16),
        scratch_shapes=[pltpu.VMEM((D_STATE, d_inner), F32),
                        pltpu.VMEM((nck, SSD_CHUNK, d_inner), F32),
                        pltpu.VMEM((nck, SSD_CHUNK, LANES), F32),
                        pltpu.VMEM((nck, LANES, SSD_CHUNK), F32),
                        pltpu.VMEM((nck, LANES, SSD_CHUNK), F32),
                        pltpu.VMEM((nck, LANES, SSD_CHUNK), F32)],
        compiler_params=_cparams(2),
        name="ssd",
    )(sz.reshape(bsz, seq, d_inner), x.reshape(bsz, seq, d_inner), bc.reshape(bsz, seq, bc_w),
      dt.reshape(bsz, seq, LANES), alog, dskip, nw)
    return out.reshape(bsz * seq, d_inner)


def _merge_kernel(x_ref, att_ref, yn_ref, gp_ref, bg_ref, wa_ref, ws_ref, wo_ref, o_ref):
    d = x_ref.shape[-1]
    ya = jnp.dot(att_ref[...], wa_ref[...], preferred_element_type=F32)
    ys = jnp.dot(yn_ref[...], ws_ref[...], preferred_element_type=F32)
    gates = _sigmoid(gp_ref[...] + bg_ref[...])
    mixed = (gates[:, :d] * ya + gates[:, d:] * ys).astype(BF16)
    o_ref[...] = x_ref[...] + jnp.dot(mixed, wo_ref[...], preferred_element_type=F32)


def _merge(x, att, yn, gates, bg, wa, ws, wo, tm=512):
    t, d = x.shape
    return pl.pallas_call(
        _merge_kernel,
        grid=(t // tm,),
        in_specs=[pl.BlockSpec((tm, d), lambda i: (i, 0)),
                  pl.BlockSpec((tm, att.shape[1]), lambda i: (i, 0)),
                  pl.BlockSpec((tm, yn.shape[1]), lambda i: (i, 0)),
                  pl.BlockSpec((tm, 2 * d), lambda i: (i, 0)),
                  _resident((1, 2 * d)), _resident(wa.shape), _resident(ws.shape), _resident(wo.shape)],
        out_specs=pl.BlockSpec((tm, d), lambda i: (i, 0)),
        out_shape=jax.ShapeDtypeStruct((t, d), F32),
        compiler_params=_cparams(1),
        name="merge",
    )(x, att, yn, gates, bg, wa, ws, wo)


def _rope_tables(seq):
    pos = jnp.arange(seq, dtype=F32)
    inv_freq = 1.0 / (ROPE_THETA ** (jnp.arange(0, HEAD_DIM, 2, dtype=F32) / HEAD_DIM))
    ang = pos[:, None] * inv_freq[None, :]
    cos, sin = jnp.cos(ang), jnp.sin(ang)
    reps = LANES // HEAD_DIM
    return (jnp.tile(jnp.concatenate([cos, cos], axis=1), (1, reps)),
            jnp.tile(jnp.concatenate([-sin, sin], axis=1), (1, reps)))


def kernel(x, ffn1_norm_w, ffn1_w_gate, ffn1_w_up, ffn1_w_down, mix_norm_w, w_in, b_gates,
           q_norm_w, k_norm_w, conv_w, conv_b, dt_bias, a_log, d_skip, ssm_norm_w,
           w_att_proj, w_ssm_proj, w_out, ffn2_norm_w, ffn2_w_gate, ffn2_w_up, ffn2_w_down):
    bsz, seq, d_model = x.shape
    depth = w_in.shape[0]
    d_inner = ssm_norm_w.shape[1]
    n_ssm_heads = dt_bias.shape[1]
    qkv_cols = len(ATT_DILATIONS) * ATT_GROUP_COLS
    bc_w = 2 * N_SSM_GROUPS * D_STATE
    assert seq % (ATT_DILATIONS[-1] * ATT_BAND) == 0 and seq % 1024 == 0
    assert seq % (SSD_CHUNK * SSD_CHUNKS_PER_STEP) == 0
    assert w_in.shape[2] == 3 * qkv_cols + d_inner + (d_inner + bc_w) + n_ssm_heads + 2 * d_model
    assert n_ssm_heads <= LANES

    o_z = 3 * qkv_cols
    o_x = o_z + d_inner
    o_bc = o_x + d_inner
    o_dt = o_bc + bc_w
    o_g = o_dt + n_ssm_heads
    gc = ATT_GROUP_COLS
    w_groups = [jnp.concatenate([w_in[:, :, s * qkv_cols + g * gc: s * qkv_cols + (g + 1) * gc]
                                 for s in range(3)], axis=2).astype(BF16)
                for g in range(len(ATT_DILATIONS))]
    tn = PROJ_TN
    assert d_inner == 2 * tn and d_model == tn and bc_w == tn
    z_w, x_w, bc_wt, g_w = w_in[:, :, o_z:o_x], w_in[:, :, o_x:o_bc], w_in[:, :, o_bc:o_dt], w_in[:, :, o_g:]
    w_pairs = jnp.concatenate([x_w[:, :, :tn], z_w[:, :, :tn], x_w[:, :, tn:], z_w[:, :, tn:],
                               bc_wt, g_w[:, :, :tn], g_w[:, :, tn:], jnp.zeros_like(bc_wt)],
                              axis=2).astype(BF16)
    pad_h = LANES - n_ssm_heads
    w_dt = jnp.pad(w_in[:, :, o_dt:o_g], ((0, 0), (0, 0), (0, pad_h))).astype(BF16)
    f1g, f1u, f1d = ffn1_w_gate.astype(BF16), ffn1_w_up.astype(BF16), ffn1_w_down.astype(BF16)
    f2g, f2u, f2d = ffn2_w_gate.astype(BF16), ffn2_w_up.astype(BF16), ffn2_w_down.astype(BF16)
    wa_b, ws_b, wo_b = w_att_proj.astype(BF16), w_ssm_proj.astype(BF16), w_out.astype(BF16)

    reps = LANES // HEAD_DIM
    head_nw = jnp.stack([jnp.tile(q_norm_w * (HEAD_DIM ** -0.5), (1, reps)),
                         jnp.tile(k_norm_w, (1, reps))], axis=1)[:, :, None, :]
    cos_t, sin_t = _rope_tables(seq)
    hid = jnp.arange(LANES) // HEAD_DIM
    block_diag = jnp.where(hid[:, None] == hid[None, :], 1.0 / HEAD_DIM, 0.0).astype(BF16)
    dtb_p = jnp.pad(dt_bias, ((0, 0), (0, pad_h)))[:, None, :]
    alog_p = jnp.pad(a_log, ((0, 0), (0, pad_h)))[:, None, :]
    dskip_e = jnp.repeat(d_skip, SSM_HEAD_DIM, axis=1)[:, None, :]

    xt = x.reshape(bsz * seq, d_model)
    for i in range(depth):
        xt, h = _ffn(xt, ffn1_norm_w[i][None], f1g[i], f1u[i], f1d[i], mix_norm_w[i][None])
        qkv = []
        for g, dil in enumerate(ATT_DILATIONS):
            qkv += _qkv(h, w_groups[g][i], head_nw[i], cos_t, sin_t, block_diag,
                        bsz=bsz, seq=seq, dil=dil)
        xc, sz, gates, bc, dt = _proj(h, w_pairs[i], w_dt[i], conv_w[i], conv_b[i][None], dtb_p[i],
                                      seq=seq, d_inner=d_inner, d_model=d_model)
        att = _attention(qkv, bsz=bsz, seq=seq).reshape(bsz * seq, ATT_GROUP_COLS)
        yn = _ssd(sz, xc, bc, dt, alog_p[i], dskip_e[i], ssm_norm_w[i][None],
                  bsz=bsz, seq=seq, d_inner=d_inner)
        xt = _merge(xt, att, yn, gates, b_gates[i][None], wa_b[i], ws_b[i], wo_b[i])
        xt = _ffn(xt, ffn2_norm_w[i][None], f2g[i], f2u[i], f2d[i])
    return xt.reshape(bsz, seq, d_model)
```
